```python
import math
import jax, jax.numpy as jnp
from jax import lax
import numpy as np

D_MODEL = 2048
BATCH = 2
SEQ = 16384
DEPTH = 1
DEC_BATCH = 8
DEC_SEQ = 16
PAST_LEN = 2048

CHUNK = 64
MIX_WIDTH = D_MODEL
DN_DK = 128
DN_DV = 128
DN_HEADS = (MIX_WIDTH // 2) // DN_DV
RET_DK = 256
RET_DV = 256
RET_HEADS = (MIX_WIDTH - DN_HEADS * DN_DV) // RET_DV
CONV_W = 4
CONV_CH = DN_HEADS * (2 * DN_DK + DN_DV)
D_FF = ((8 * D_MODEL + 3 * 256 - 1) // (3 * 256)) * 256
ROPE_BASE = 10000.0
EPS = 1e-6
N_ADA = 6
SPLIT_SIZES = (CONV_CH, DN_HEADS * DN_DV, DN_HEADS, DN_HEADS,
               RET_HEADS * RET_DK, RET_HEADS * RET_DK, RET_HEADS * RET_DV, RET_HEADS * RET_DV)
PROJ_WIDTH = CONV_CH + DN_HEADS * DN_DV + 2 * DN_HEADS + 2 * RET_HEADS * RET_DK + 2 * RET_HEADS * RET_DV

kernel_name = 'hybrid_gdn_retention_stream'

F32 = jnp.float32


def rmsnorm(x, w=None):
    xf = x.astype(F32)
    y = xf * lax.rsqrt(jnp.mean(xf * xf, axis=-1, keepdims=True) + EPS)
    if w is not None:
        y = y * w.astype(F32)
    return y


def modulate(h, shift, scale):
    return h * (1.0 + scale[:, None, :]) + shift[:, None, :]


def l2norm(x):
    return x * lax.rsqrt(jnp.sum(x * x, axis=-1, keepdims=True) + EPS)


def rotary(x, pos):
    d = x.shape[-1]
    inv = 1.0 / (ROPE_BASE ** jnp.linspace(0.0, 1.0, d // 2, dtype=F32))
    ang = pos.astype(F32)[:, None] * inv[None, :]
    cos = jnp.cos(ang)[None, :, None, :]
    sin = jnp.sin(ang)[None, :, None, :]
    x1, x2 = x[..., 0::2], x[..., 1::2]
    return jnp.stack([x1 * cos - x2 * sin, x2 * cos + x1 * sin], axis=-1).reshape(x.shape)


def to_blocks(a, chunk):
    b, t = a.shape[:2]
    a = a.reshape((b, t // chunk, chunk) + a.shape[2:])
    return jnp.moveaxis(a, 3, 2)


def from_blocks(o):
    o = jnp.moveaxis(o, 2, 3)
    b, n, c = o.shape[:3]
    return o.reshape((b, n * c) + o.shape[3:])


def gated_delta_chunked(q, k, v, g, beta, s0, chunk):
    dv = v.shape[-1]
    qb, kb, vb = to_blocks(q, chunk), to_blocks(k, chunk), to_blocks(v, chunk)
    gb, bb = to_blocks(g, chunk), to_blocks(beta, chunk)
    decay = jnp.cumsum(gb, axis=-1)
    causal = jnp.tril(jnp.ones((chunk, chunk), bool))
    strict = jnp.tril(jnp.ones((chunk, chunk), bool), k=-1)
    diff = decay[..., :, None] - decay[..., None, :]
    gam = jnp.where(causal, jnp.exp(jnp.where(causal, diff, 0.0)), 0.0)
    k_beta = kb * bb[..., None]
    m = jnp.where(strict, jnp.einsum('bnhid,bnhjd->bnhij', k_beta, kb) * gam, 0.0)
    eye = jnp.eye(chunk, dtype=F32)
    rhs = jnp.concatenate([vb * bb[..., None], k_beta * jnp.exp(decay)[..., None]], axis=-1)
    sol = lax.linalg.triangular_solve(m + eye, rhs, left_side=True, lower=True, unit_diagonal=True)
    u_base, w_dec = sol[..., :dv], sol[..., dv:]
    qk = jnp.einsum('bnhid,bnhjd->bnhij', qb, kb) * gam
    q_dec = qb * jnp.exp(decay)[..., None]
    last = decay[..., -1:]
    k_dec = kb * jnp.exp(last - decay)[..., None]
    chunk_decay = jnp.exp(last[..., 0])

    def step(s, xs):
        u_b, w_c, qk_c, qd, kd, cd = xs
        u = u_b - jnp.einsum('bhck,bhkv->bhcv', w_c, s)
        o = jnp.einsum('bhck,bhkv->bhcv', qd, s) + jnp.einsum('bhij,bhjv->bhiv', qk_c, u)
        s = s * cd[..., None, None] + jnp.einsum('bhck,bhcv->bhkv', kd, u)
        return s, o

    xs = tuple(jnp.moveaxis(a, 1, 0) for a in (u_base, w_dec, qk, q_dec, k_dec, chunk_decay))
    s_fin, o = lax.scan(step, s0, xs)
    return from_blocks(jnp.moveaxis(o, 0, 1)), s_fin


def retention_chunked(q, k, v, log_gamma, s0, chunk):
    qb, kb, vb = to_blocks(q, chunk), to_blocks(k, chunk), to_blocks(v, chunk)
    idx = jnp.arange(chunk, dtype=F32)
    diff = idx[:, None] - idx[None, :]
    lg = log_gamma[:, None, None]
    dmask = jnp.where(diff >= 0, jnp.exp(jnp.where(diff >= 0, diff, 0.0) * lg), 0.0)
    inner = jnp.einsum('bnhij,bnhjv->bnhiv', jnp.einsum('bnhid,bnhjd->bnhij', qb, kb) * dmask, vb)
    xi = jnp.exp((idx + 1.0)[None, :] * log_gamma[:, None])
    zeta = jnp.exp((chunk - 1.0 - idx)[None, :] * log_gamma[:, None])
    cd = jnp.exp(chunk * log_gamma)[None, :, None, None]
    q_x = qb * xi[..., None]
    k_z = kb * zeta[..., None]

    def step(s, xs):
        inn, qx, kz, vv = xs
        o = inn + jnp.einsum('bhck,bhkv->bhcv', qx, s)
        s = s * cd + jnp.einsum('bhck,bhcv->bhkv', kz, vv)
        return s, o

    xs = tuple(jnp.moveaxis(a, 1, 0) for a in (inner, q_x, k_z, vb))
    s_fin, o = lax.scan(step, s0, xs)
    return from_blocks(jnp.moveaxis(o, 0, 1)), s_fin


def block(x, c, conv_st, dn_st, ret_st, pos, chunk, norm_mix, norm_ffn, w_ada, b_ada, w_in, conv_w,
          dn_a_log, dn_dt_bias, dn_norm, w_out, w_gu, w_down):
    dt = x.dtype
    b, t, _ = x.shape
    ada = (jax.nn.silu(c) @ w_ada + b_ada).astype(F32)
    sh_m, sc_m, g_m, sh_f, sc_f, g_f = jnp.split(ada, N_ADA, axis=-1)

    h = modulate(rmsnorm(x, norm_mix), sh_m, sc_m).astype(dt)
    p = (h @ w_in).astype(F32)
    points = np.cumsum(SPLIT_SIZES)[:-1].tolist()
    qkv_pre, z, b_raw, a_raw, rq, rk, rv, rg = jnp.split(p, points, axis=-1)

    xcat = jnp.concatenate([conv_st.astype(F32), qkv_pre], axis=1)
    new_conv = xcat[:, -(CONV_W - 1):].astype(conv_st.dtype)
    wc = conv_w.astype(F32)
    conv = xcat[:, 0:t] * wc[0]
    for i in range(1, CONV_W):
        conv = conv + xcat[:, i:i + t] * wc[i]
    qkv = jax.nn.silu(conv)
    dq, dk, dv = jnp.split(qkv, [DN_HEADS * DN_DK, 2 * DN_HEADS * DN_DK], axis=-1)
    dq = l2norm(dq.reshape(b, t, DN_HEADS, DN_DK)) * (DN_DK ** -0.5)
    dk = l2norm(dk.reshape(b, t, DN_HEADS, DN_DK))
    dv = dv.reshape(b, t, DN_HEADS, DN_DV)
    beta = jax.nn.sigmoid(b_raw)
    g = -jnp.exp(dn_a_log.astype(F32)) * jax.nn.softplus(a_raw + dn_dt_bias.astype(F32))
    o_dn, new_dn = gated_delta_chunked(dq, dk, dv, g, beta, dn_st.astype(F32), chunk)
    o_dn = rmsnorm(o_dn, dn_norm) * jax.nn.silu(z.reshape(b, t, DN_HEADS, DN_DV))

    rq = rotary(rq.reshape(b, t, RET_HEADS, RET_DK), pos)
    rk = rotary(rk.reshape(b, t, RET_HEADS, RET_DK), pos) * (RET_DK ** -0.5)
    rv = rv.reshape(b, t, RET_HEADS, RET_DV)
    log_gamma = jnp.log(1.0 - 2.0 ** (-5.0 - jnp.arange(RET_HEADS, dtype=F32)))
    o_r, new_ret = retention_chunked(rq, rk, rv, log_gamma, ret_st.astype(F32), chunk)
    o_r = rmsnorm(o_r) * jax.nn.silu(rg.reshape(b, t, RET_HEADS, RET_DV))

    mixed = jnp.concatenate([o_dn.reshape(b, t, -1), o_r.reshape(b, t, -1)], axis=-1).astype(dt)
    x = (x.astype(F32) + g_m[:, None, :] * (mixed @ w_out).astype(F32)).astype(dt)

    h = modulate(rmsnorm(x, norm_ffn), sh_f, sc_f).astype(dt)
    gate, up = jnp.split(h @ w_gu, 2, axis=-1)
    f = (jax.nn.silu(gate) * up) @ w_down
    x = (x.astype(F32) + g_f[:, None, :] * f.astype(F32)).astype(dt)
    return x, new_conv, new_dn.astype(dn_st.dtype), new_ret.astype(ret_st.dtype)


def setup_inputs(seed: int = 0) -> dict:
    key = jax.random.key(seed)
    ks = jax.random.split(key, 24)

    def nrm(k, shape, scale):
        return jax.random.normal(k, shape, F32) * scale

    dt_min, dt_max = 0.001, 0.1
    dt = jnp.exp(jax.random.uniform(ks[13], (DEPTH, DN_HEADS), F32) * (math.log(dt_max) - math.log(dt_min)) + math.log(dt_min))
    return {
        'x_prompt': nrm(ks[0], (BATCH, SEQ, D_MODEL), 1.0),
        'x_sample': nrm(ks[1], (DEC_BATCH, DEC_SEQ, D_MODEL), 1.0),
        'state_conv': nrm(ks[2], (DEPTH, DEC_BATCH, CONV_W - 1, CONV_CH), 1.0),
        'state_delta': nrm(ks[3], (DEPTH, DEC_BATCH, DN_HEADS, DN_DK, DN_DV), 0.1),
        'state_ret': nrm(ks[4], (DEPTH, DEC_BATCH, RET_HEADS, RET_DK, RET_DV), 0.5),
        'c_prompt': nrm(ks[5], (BATCH, D_MODEL), 1.0),
        'c_sample': nrm(ks[6], (DEC_BATCH, D_MODEL), 1.0),
        'norm_mix': 1.0 + nrm(ks[7], (DEPTH, D_MODEL), 0.02),
        'norm_ffn': 1.0 + nrm(ks[8], (DEPTH, D_MODEL), 0.02),
        'w_ada': nrm(ks[9], (DEPTH, D_MODEL, N_ADA * D_MODEL), 0.5 * D_MODEL ** -0.5),
        'b_ada': nrm(ks[10], (DEPTH, N_ADA * D_MODEL), 0.01),
        'w_in': nrm(ks[11], (DEPTH, D_MODEL, PROJ_WIDTH), D_MODEL ** -0.5),
        'conv_w': nrm(ks[12], (DEPTH, CONV_W, CONV_CH), CONV_W ** -0.5),
        'dn_a_log': jnp.log(jax.random.uniform(ks[14], (DEPTH, DN_HEADS), F32, 1.0, 16.0)),
        'dn_dt_bias': dt + jnp.log(-jnp.expm1(-dt)),
        'dn_norm': 1.0 + nrm(ks[15], (DEPTH, DN_DV), 0.02),
        'w_out': nrm(ks[16], (DEPTH, MIX_WIDTH, D_MODEL), MIX_WIDTH ** -0.5),
        'w_gu': nrm(ks[17], (DEPTH, D_MODEL, 2 * D_FF), D_MODEL ** -0.5),
        'w_down': nrm(ks[18], (DEPTH, D_FF, D_MODEL), D_FF ** -0.5),
        'norm_final': 1.0 + nrm(ks[19], (D_MODEL,), 0.02),
        'w_ada_final': nrm(ks[20], (D_MODEL, 2 * D_MODEL), 0.5 * D_MODEL ** -0.5),
        'b_ada_final': nrm(ks[21], (2 * D_MODEL,), 0.01),
    }


def reference(x_prompt, x_sample, state_conv, state_delta, state_ret, c_prompt, c_sample,
              norm_mix, norm_ffn, w_ada, b_ada, w_in, conv_w, dn_a_log, dn_dt_bias, dn_norm,
              w_out, w_gu, w_down, norm_final, w_ada_final, b_ada_final):
    bp, tp = x_prompt.shape[0], x_prompt.shape[1]
    ts = x_sample.shape[1]
    pos_p = jnp.arange(tp)
    pos_s = PAST_LEN + jnp.arange(ts)
    zc = jnp.zeros((bp, CONV_W - 1, CONV_CH), x_prompt.dtype)
    zd = jnp.zeros((bp, DN_HEADS, DN_DK, DN_DV), state_delta.dtype)
    zr = jnp.zeros((bp, RET_HEADS, RET_DK, RET_DV), state_ret.dtype)
    hp, hs = x_prompt, x_sample
    cp_l, dp_l, rp_l, cs_l, ds_l, rs_l = [], [], [], [], [], []
    for l in range(DEPTH):
        hp, cp, dp, rp = block(hp, c_prompt, zc, zd, zr, pos_p, CHUNK, norm_mix[l], norm_ffn[l],
                               w_ada[l], b_ada[l], w_in[l], conv_w[l], dn_a_log[l], dn_dt_bias[l],
                               dn_norm[l], w_out[l], w_gu[l], w_down[l])
        hs, cs, ds, rs = block(hs, c_sample, state_conv[l], state_delta[l], state_ret[l], pos_s, ts,
                               norm_mix[l], norm_ffn[l], w_ada[l], b_ada[l], w_in[l], conv_w[l],
                               dn_a_log[l], dn_dt_bias[l], dn_norm[l], w_out[l], w_gu[l], w_down[l])
        cp_l.append(cp); dp_l.append(dp); rp_l.append(rp)
        cs_l.append(cs); ds_l.append(ds); rs_l.append(rs)

    def final(h, c):
        ada = (jax.nn.silu(c) @ w_ada_final + b_ada_final).astype(F32)
        shift, scale = jnp.split(ada, 2, axis=-1)
        return modulate(rmsnorm(h, norm_final), shift, scale).astype(h.dtype)

    y_prompt = final(hp, c_prompt)
    y_sample = final(hs, c_sample)
    return (y_prompt, y_sample, jnp.stack(cp_l), jnp.stack(dp_l), jnp.stack(rp_l),
            jnp.stack(cs_l), jnp.stack(ds_l), jnp.stack(rs_l))
```

```python
import functools
import math

import numpy as np
import jax
import jax.numpy as jnp
from jax import lax
from jax.experimental import pallas as pl
from jax.experimental.pallas import tpu as pltpu

F32 = jnp.float32
BF16 = jnp.bfloat16

DN_DK = 128
DN_DV = 128
RET_DK = 256
RET_DV = 256
CONV_W = 4
N_ADA = 6
CHUNK = 64
PAST_LEN = 2048
ROPE_BASE = 10000.0
EPS = 1e-6

LANES = 128
SUBLANES = 8
VMEM_LIMIT = 56 * 1024 * 1024

HIGHEST = lax.Precision.HIGHEST


def _silu(x):
    return x * jax.nn.sigmoid(x)


def _dot(a, b):
    return jnp.dot(a.astype(BF16), b.astype(BF16), preferred_element_type=F32)


def _dot_nt(a, b):
    return lax.dot_general(a.astype(BF16), b.astype(BF16), (((1,), (1,)), ((), ())),
                           preferred_element_type=F32)


def _dot_tn(a, b):
    return lax.dot_general(a.astype(BF16), b.astype(BF16), (((0,), (0,)), ((), ())),
                           preferred_element_type=F32)


def _dot_hi(a, b):
    return jnp.dot(a, b, preferred_element_type=F32, precision=HIGHEST)


def _params(*sem):
    return pltpu.CompilerParams(dimension_semantics=sem, vmem_limit_bytes=VMEM_LIMIT)


def _ada_kernel(c_ref, w_ref, b_ref, o_ref):
    s = _silu(c_ref[...])
    o_ref[...] = _dot(s, w_ref[...]) + b_ref[...]


def _ada(c_all, w, b, tn):
    m, d = c_all.shape
    n = w.shape[1]
    return pl.pallas_call(
        _ada_kernel,
        grid=(n // tn,),
        in_specs=[pl.BlockSpec((m, d), lambda j: (0, 0)),
                  pl.BlockSpec((d, tn), lambda j: (0, j)),
                  pl.BlockSpec((1, tn), lambda j: (0, j))],
        out_specs=pl.BlockSpec((m, tn), lambda j: (0, j)),
        out_shape=jax.ShapeDtypeStruct((m, n), F32),
        compiler_params=_params("arbitrary"),
        name="ada",
    )(c_all, w, b.reshape(1, n))


def _norm_mod(x, nw, shift, scale):
    y = x * lax.rsqrt(jnp.mean(x * x, axis=-1, keepdims=True) + EPS)
    y = y * nw
    return y * (1.0 + scale) + shift


def _proj_kernel(x_ref, nw_ref, sh_ref, sc_ref, w_ref, wg_ref, p_ref, pg_ref, h_ref):
    @pl.when(pl.program_id(1) == 0)
    def _():
        h = _norm_mod(x_ref[...], nw_ref[...], sh_ref[0], sc_ref[0])
        h_ref[...] = h.astype(BF16)
        pg_ref[...] = jnp.dot(h_ref[...], wg_ref[...], preferred_element_type=F32)

    p_ref[...] = jnp.dot(h_ref[...], w_ref[...], preferred_element_type=F32)


def _in_proj(x, nw, sh, sc, mod_map, w_main, w_gate, tm, tn):
    m, d = x.shape
    n = w_main.shape[1]
    r = sh.shape[1]
    return pl.pallas_call(
        _proj_kernel,
        grid=(m // tm, n // tn),
        in_specs=[pl.BlockSpec((tm, d), lambda i, j: (i, 0)),
                  pl.BlockSpec((1, d), lambda i, j: (0, 0)),
                  pl.BlockSpec((1, r, d), lambda i, j: (mod_map(i, 0), 0, 0)),
                  pl.BlockSpec((1, r, d), lambda i, j: (mod_map(i, 1), 0, 0)),
                  pl.BlockSpec((d, tn), lambda i, j: (0, j)),
                  pl.BlockSpec((d, LANES), lambda i, j: (0, 0))],
        out_specs=[pl.BlockSpec((tm, tn), lambda i, j: (i, j)),
                   pl.BlockSpec((tm, LANES), lambda i, j: (i, 0))],
        out_shape=[jax.ShapeDtypeStruct((m, n), F32),
                   jax.ShapeDtypeStruct((m, LANES), F32)],
        scratch_shapes=[pltpu.VMEM((tm, d), BF16)],
        compiler_params=_params("parallel", "arbitrary"),
        name="in_proj",
    )(x, nw, sh, sc, w_main, w_gate)


def _unit_lower_inverse(m, eye):
    c = m.shape[0]
    inv = eye - m
    pw = m
    k = 2
    while k < c:
        pw = _dot_hi(pw, pw)
        inv = inv + _dot_hi(inv, pw)
        k *= 2
    return inv


def _mixer_kernel(lg_ref, p_ref, pg_ref, cw_ref, gp_ref, dnw_ref, inv_ref,
                  conv0_ref, dn0_ref, ret0_ref,
                  mix_ref, conv_ref, dn_ref, ret_ref, xe_ref,
                  *, chunk, dn_heads, ret_heads, pos0):
    c = chunk
    n = pl.program_id(1)
    conv_ch = dn_heads * (2 * DN_DK + DN_DV)
    tail = SUBLANES

    @pl.when(n == 0)
    def _():
        xe_ref[0:tail, :] = jnp.zeros((tail, conv_ch), F32)
        xe_ref[tail - (CONV_W - 1):tail, :] = conv0_ref[0]
        dn_ref[...] = dn0_ref[...]
        ret_ref[...] = ret0_ref[...]

    @pl.when(n > 0)
    def _():
        xe_ref[0:tail, :] = xe_ref[c:c + tail, :]

    xe_ref[tail:tail + c, :] = p_ref[:, 0:conv_ch]

    @pl.when(n == pl.num_programs(1) - 1)
    def _():
        conv_ref[0] = xe_ref[tail + c - (CONV_W - 1):tail + c, :]

    row = lax.broadcasted_iota(jnp.int32, (c, c), 0)
    col = lax.broadcasted_iota(jnp.int32, (c, c), 1)
    causal = row >= col
    strict = row > col
    eye = (row == col).astype(F32)
    tril = causal.astype(F32)

    pg = pg_ref[...]
    beta_all = jax.nn.sigmoid(pg)
    g_all = gp_ref[0:1, :] * jax.nn.softplus(pg + gp_ref[1:2, :])
    decay_all = _dot_hi(tril, g_all)
    decay_t = decay_all.T

    def conv_silu(off):
        acc = xe_ref[pl.ds(tail, c), off:off + LANES] * cw_ref[CONV_W - 1:CONV_W, off:off + LANES]
        for i in range(1, CONV_W):
            acc = acc + (xe_ref[pl.ds(tail - i, c), off:off + LANES]
                         * cw_ref[CONV_W - 1 - i:CONV_W - i, off:off + LANES])
        return _silu(acc)

    z_off = conv_ch
    for h in range(dn_heads):
        q = conv_silu(h * DN_DK)
        k = conv_silu(dn_heads * DN_DK + h * DN_DK)
        v = conv_silu(2 * dn_heads * DN_DK + h * DN_DV)
        q = q * lax.rsqrt(jnp.sum(q * q, axis=-1, keepdims=True) + EPS) * (DN_DK ** -0.5)
        k = k * lax.rsqrt(jnp.sum(k * k, axis=-1, keepdims=True) + EPS)
        beta = beta_all[:, h:h + 1]
        dcol = decay_all[:, dn_heads + h:dn_heads + h + 1]
        drow = decay_t[dn_heads + h:dn_heads + h + 1, :]
        last = drow[:, c - 1:c]
        gam = jnp.where(causal, jnp.exp(jnp.where(causal, dcol - drow, 0.0)), 0.0)
        kb = k * beta
        mm = jnp.where(strict, _dot_nt(kb, k) * gam, 0.0)
        tinv = _unit_lower_inverse(mm, eye)
        edec = jnp.exp(dcol)
        rhs = jnp.concatenate([v * beta, kb * edec], axis=-1)
        sol = _dot_hi(tinv, rhs)
        u_base = sol[:, :DN_DV]
        w_dec = sol[:, DN_DV:]
        qk = _dot_nt(q, k) * gam
        q_dec = q * edec
        k_dec = k * jnp.exp(last - dcol)
        s = dn_ref[0, h]
        sb = s.astype(BF16)
        u = u_base - _dot(w_dec, sb)
        o = _dot(q_dec, sb) + _dot(qk, u)
        dn_ref[0, h] = s * jnp.exp(last) + _dot_tn(k_dec, u)
        o = o * lax.rsqrt(jnp.mean(o * o, axis=-1, keepdims=True) + EPS) * dnw_ref[...]
        zz = p_ref[:, z_off + h * DN_DV:z_off + (h + 1) * DN_DV]
        mix_ref[:, h * DN_DV:(h + 1) * DN_DV] = (o * _silu(zz)).astype(mix_ref.dtype)

    half = RET_DK // 2
    rq_off = conv_ch + dn_heads * DN_DV
    rk_off = rq_off + ret_heads * RET_DK
    rv_off = rk_off + ret_heads * RET_DK
    rg_off = rv_off + ret_heads * RET_DV
    mix_off = dn_heads * DN_DV
    tok = lax.broadcasted_iota(jnp.int32, (c, 1), 0)
    pos = (pos0 + n * c + tok).astype(F32)
    ang = pos * inv_ref[...]
    cos = jnp.cos(ang)
    sin = jnp.sin(ang)
    idx = tok.astype(F32)
    dpos = (row - col).astype(F32)

    def rot(off):
        x1 = p_ref[:, off:off + half]
        x2 = p_ref[:, off + half:off + 2 * half]
        return jnp.concatenate([x1 * cos - x2 * sin, x2 * cos + x1 * sin], axis=-1)

    for h in range(ret_heads):
        lg = lg_ref[h]
        rq = rot(rq_off + h * RET_DK)
        rk = rot(rk_off + h * RET_DK) * (RET_DK ** -0.5)
        rv = p_ref[:, rv_off + h * RET_DV:rv_off + (h + 1) * RET_DV]
        dmask = jnp.where(causal, jnp.exp(jnp.where(causal, dpos, 0.0) * lg), 0.0)
        inner = _dot(_dot_nt(rq, rk) * dmask, rv)
        xi = jnp.exp((idx + 1.0) * lg)
        zeta = jnp.exp((c - 1.0 - idx) * lg)
        s = ret_ref[0, h]
        o = inner + _dot(rq * xi, s)
        cd = jnp.exp(jnp.zeros((1, 1), F32) + c * lg)
        ret_ref[0, h] = s * cd + _dot_tn(rk * zeta, rv)
        o = o * lax.rsqrt(jnp.mean(o * o, axis=-1, keepdims=True) + EPS)
        gg = p_ref[:, rg_off + h * RET_DV:rg_off + (h + 1) * RET_DV]
        mix_ref[:, mix_off + h * RET_DV:mix_off + (h + 1) * RET_DV] = (
            (o * _silu(gg)).astype(mix_ref.dtype))


def _mixer(p_main, p_gate, conv_w, gate_par, dn_norm, inv_freq, log_gamma,
           conv0, dn0, ret0, *, batch, chunk, pos0):
    m, width = p_main.shape
    nchunk = m // (batch * chunk)
    dn_heads = dn0.shape[1]
    ret_heads = ret0.shape[1]
    conv_ch = conv_w.shape[1]
    mix_w = dn_heads * DN_DV + ret_heads * RET_DV
    kern = functools.partial(_mixer_kernel, chunk=chunk, dn_heads=dn_heads,
                             ret_heads=ret_heads, pos0=pos0)
    full = lambda shape: pl.BlockSpec(shape, lambda b, n: (0,) * len(shape))
    return pl.pallas_call(
        kern,
        grid=(batch, nchunk),
        in_specs=[pl.BlockSpec(memory_space=pltpu.SMEM),
                  pl.BlockSpec((chunk, width), lambda b, n: (b * nchunk + n, 0)),
                  pl.BlockSpec((chunk, LANES), lambda b, n: (b * nchunk + n, 0)),
                  full(conv_w.shape), full(gate_par.shape), full(dn_norm.shape),
                  full(inv_freq.shape),
                  pl.BlockSpec((1,) + conv0.shape[1:], lambda b, n: (b, 0, 0)),
                  pl.BlockSpec((1,) + dn0.shape[1:], lambda b, n: (b, 0, 0, 0)),
                  pl.BlockSpec((1,) + ret0.shape[1:], lambda b, n: (b, 0, 0, 0))],
        out_specs=[pl.BlockSpec((chunk, mix_w), lambda b, n: (b * nchunk + n, 0)),
                   pl.BlockSpec((1,) + conv0.shape[1:], lambda b, n: (b, 0, 0)),
                   pl.BlockSpec((1,) + dn0.shape[1:], lambda b, n: (b, 0, 0, 0)),
                   pl.BlockSpec((1,) + ret0.shape[1:], lambda b, n: (b, 0, 0, 0))],
        out_shape=[jax.ShapeDtypeStruct((m, mix_w), BF16),
                   jax.ShapeDtypeStruct(conv0.shape, F32),
                   jax.ShapeDtypeStruct(dn0.shape, F32),
                   jax.ShapeDtypeStruct(ret0.shape, F32)],
        scratch_shapes=[pltpu.VMEM((chunk + SUBLANES, conv_ch), F32)],
        compiler_params=_params("parallel", "arbitrary"),
        name="mixer",
    )(log_gamma, p_main, p_gate, conv_w, gate_par, dn_norm, inv_freq, conv0, dn0, ret0)


def _out_kernel(x_ref, mix_ref, gm_ref, w_ref, o_ref):
    o_ref[...] = x_ref[...] + gm_ref[0] * jnp.dot(mix_ref[...], w_ref[...],
                                                  preferred_element_type=F32)


def _out_proj(x, mixed, gm, mod_map, w_out, tm):
    m, d = x.shape
    k = mixed.shape[1]
    r = gm.shape[1]
    return pl.pallas_call(
        _out_kernel,
        grid=(m // tm,),
        in_specs=[pl.BlockSpec((tm, d), lambda i: (i, 0)),
                  pl.BlockSpec((tm, k), lambda i: (i, 0)),
                  pl.BlockSpec((1, r, d), lambda i: (mod_map(i, 2), 0, 0)),
                  pl.BlockSpec((k, d), lambda i: (0, 0))],
        out_specs=pl.BlockSpec((tm, d), lambda i: (i, 0)),
        out_shape=jax.ShapeDtypeStruct((m, d), F32),
        compiler_params=_params("parallel"),
        name="out_proj",
    )(x, mixed, gm, w_out)


def _ffn_kernel(x_ref, nw_ref, sh_ref, sc_ref, gf_ref, wg_ref, wu_ref, wd_ref,
                nf_ref, shf_ref, scf_ref, y_ref, h_ref):
    j = pl.program_id(1)

    @pl.when(j == 0)
    def _():
        h = _norm_mod(x_ref[...], nw_ref[...], sh_ref[0], sc_ref[0])
        h_ref[...] = h.astype(BF16)
        y_ref[...] = jnp.zeros_like(y_ref)

    hb = h_ref[...]
    gate = jnp.dot(hb, wg_ref[...], preferred_element_type=F32)
    up = jnp.dot(hb, wu_ref[...], preferred_element_type=F32)
    act = (_silu(gate) * up).astype(BF16)
    y_ref[...] += jnp.dot(act, wd_ref[...], preferred_element_type=F32)

    @pl.when(j == pl.num_programs(1) - 1)
    def _():
        x2 = x_ref[...] + gf_ref[0] * y_ref[...]
        y_ref[...] = _norm_mod(x2, nf_ref[...], shf_ref[0], scf_ref[0])


def _ffn(x, nw, ada, mod_map, w_gu, w_down, nf, ada_fin, fin_map, tm, tf):
    m, d = x.shape
    dff = w_down.shape[0]
    nf_blocks = dff // tf
    r = ada.shape[1]
    mod = lambda k: pl.BlockSpec((1, r, d), lambda i, j: (mod_map(i, k), 0, 0))
    fin = lambda k: pl.BlockSpec((1, r, d), lambda i, j: (fin_map(i, k), 0, 0))
    return pl.pallas_call(
        _ffn_kernel,
        grid=(m // tm, nf_blocks),
        in_specs=[pl.BlockSpec((tm, d), lambda i, j: (i, 0)),
                  pl.BlockSpec((1, d), lambda i, j: (0, 0)),
                  mod(3), mod(4), mod(5),
                  pl.BlockSpec((d, tf), lambda i, j: (0, j)),
                  pl.BlockSpec((d, tf), lambda i, j: (0, j + nf_blocks)),
                  pl.BlockSpec((tf, d), lambda i, j: (j, 0)),
                  pl.BlockSpec((1, d), lambda i, j: (0, 0)),
                  fin(0), fin(1)],
        out_specs=pl.BlockSpec((tm, d), lambda i, j: (i, 0)),
        out_shape=jax.ShapeDtypeStruct((m, d), F32),
        scratch_shapes=[pltpu.VMEM((tm, d), BF16)],
        compiler_params=_params("parallel", "arbitrary"),
        name="ffn",
    )(x, nw, ada, ada, ada, w_gu, w_gu, w_down, nf, ada_fin, ada_fin)


def _ret_perm(ret_heads):
    per_head = np.concatenate([np.arange(0, RET_DK, 2), np.arange(1, RET_DK, 2)])
    return np.concatenate([h * RET_DK + per_head for h in range(ret_heads)])


def _stream(x, c_rows, ada, ada_fin, conv0, dn0, ret0, weights, *, chunk, pos0, tm, tf, per_row):
    (nw_mix, nw_ffn, w_main, w_gate, conv_w, gate_par, dn_norm, inv_freq, log_gamma,
     w_out, w_gu, w_down, nf) = weights
    b, t, d = x.shape
    m = b * t
    xf = x.reshape(m, d)
    if per_row:
        ada_r = jnp.repeat(ada.reshape(b, N_ADA, d), t, axis=0).transpose(1, 0, 2)
        fin_r = jnp.repeat(ada_fin.reshape(b, 2, d), t, axis=0).transpose(1, 0, 2)
        mod_map = lambda i, k: k
        fin_map = lambda i, k: k
    else:
        tiles = t // tm
        ada_r = ada.reshape(b * N_ADA, 1, d)
        fin_r = ada_fin.reshape(b * 2, 1, d)
        mod_map = lambda i, k: (i // tiles) * N_ADA + k
        fin_map = lambda i, k: (i // tiles) * 2 + k

    p_main, p_gate = _in_proj(xf, nw_mix, ada_r, ada_r, mod_map, w_main, w_gate, tm, 1024)
    mixed, conv_n, dn_n, ret_n = _mixer(p_main, p_gate, conv_w, gate_par, dn_norm, inv_freq,
                                        log_gamma, conv0, dn0, ret0,
                                        batch=b, chunk=chunk, pos0=pos0)
    x1 = _out_proj(xf, mixed, ada_r, mod_map, w_out, tm)
    y = _ffn(x1, nw_ffn, ada_r, mod_map, w_gu, w_down, nf, fin_r, fin_map, tm, tf)
    return y.reshape(b, t, d), conv_n, dn_n, ret_n


def kernel(x_prompt, x_sample, state_conv, state_delta, state_ret, c_prompt, c_sample, norm_mix, norm_ffn, w_ada, b_ada, w_in, conv_w, dn_a_log, dn_dt_bias, dn_norm, w_out, w_gu, w_down, norm_final, w_ada_final, b_ada_final):
    bp, tp, d = x_prompt.shape
    bs, ts, _ = x_sample.shape
    depth = w_in.shape[0]
    assert depth == 1, "single-layer pipeline"
    dn_heads = state_delta.shape[2]
    ret_heads = state_ret.shape[2]
    conv_ch = conv_w.shape[2]
    assert 2 * dn_heads <= LANES

    nb = bp + bs
    pad = (-nb) % (2 * SUBLANES)
    c_all = jnp.concatenate([c_prompt, c_sample, jnp.zeros((pad, d), F32)], axis=0)
    ada_all = _ada(c_all, w_ada[0], b_ada[0], 1024)
    fin_all = _ada(c_all, w_ada_final, b_ada_final, 1024)

    l = 0
    g_off = conv_ch + dn_heads * DN_DV
    r_off = g_off + 2 * dn_heads
    perm = _ret_perm(ret_heads)
    w_l = w_in[l]
    rq_w = w_l[:, r_off:r_off + ret_heads * RET_DK][:, perm]
    rk_w = w_l[:, r_off + ret_heads * RET_DK:r_off + 2 * ret_heads * RET_DK][:, perm]
    w_main = jnp.concatenate([w_l[:, :g_off], rq_w, rk_w,
                              w_l[:, r_off + 2 * ret_heads * RET_DK:]], axis=1).astype(BF16)
    w_gate = jnp.concatenate([w_l[:, g_off:r_off],
                              jnp.zeros((d, LANES - 2 * dn_heads), F32)], axis=1).astype(BF16)
    lane_pad = jnp.zeros((LANES - 2 * dn_heads,), F32)
    gate_par = jnp.stack([
        jnp.concatenate([jnp.zeros((dn_heads,), F32), -jnp.exp(dn_a_log[l].astype(F32)), lane_pad]),
        jnp.concatenate([jnp.zeros((dn_heads,), F32), dn_dt_bias[l].astype(F32), lane_pad])])
    inv_freq = (1.0 / (ROPE_BASE ** jnp.linspace(0.0, 1.0, RET_DK // 2, dtype=F32))).reshape(1, -1)
    log_gamma = jnp.log(1.0 - 2.0 ** (-5.0 - jnp.arange(ret_heads, dtype=F32)))
    weights = (norm_mix[l].reshape(1, d), norm_ffn[l].reshape(1, d), w_main, w_gate, conv_w[l],
               gate_par, dn_norm[l].reshape(1, -1), inv_freq, log_gamma,
               w_out[l].astype(BF16), w_gu[l].astype(BF16), w_down[l].astype(BF16),
               norm_final.reshape(1, d))

    inv_perm = np.argsort(perm[:RET_DK])
    zc = jnp.zeros((bp, CONV_W - 1, conv_ch), F32)
    zd = jnp.zeros((bp,) + state_delta.shape[2:], F32)
    zr = jnp.zeros((bp,) + state_ret.shape[2:], F32)

    y_p, cp, dp, rp = _stream(x_prompt, c_prompt, ada_all[:bp], fin_all[:bp], zc, zd, zr, weights,
                              chunk=CHUNK, pos0=0, tm=512, tf=512, per_row=False)
    rs_in = state_ret[l][:, :, perm[:RET_DK], :]
    y_s, cs, ds, rs = _stream(x_sample, c_sample, ada_all[bp:nb], fin_all[bp:nb], state_conv[l],
                              state_delta[l], rs_in, weights,
                              chunk=ts, pos0=PAST_LEN, tm=bs * ts, tf=512, per_row=True)
    rp = rp[:, :, inv_perm, :]
    rs = rs[:, :, inv_perm, :]
    return (y_p, y_s, cp[None], dp[None], rp[None], cs[None], ds[None], rs[None])
```

```python
import functools
import math

import numpy as np
import jax
import jax.numpy as jnp
from jax import lax
from jax.experimental import pallas as pl
from jax.experimental.pallas import tpu as pltpu

F32 = jnp.float32
BF16 = jnp.bfloat16

DN_DK = 128
DN_DV = 128
RET_DK = 256
RET_DV = 256
CONV_W = 4
N_ADA = 6
CHUNK = 64
PAST_LEN = 2048
ROPE_BASE = 10000.0
EPS = 1e-6

LANES = 128
SUBLANES = 8
VMEM_LIMIT = 56 * 1024 * 1024


def _silu(x):
    return x * jax.nn.sigmoid(x)


def _dot(a, b):
    return jnp.dot(a.astype(BF16), b.astype(BF16), preferred_element_type=F32)


def _dot_nt(a, b):
    return lax.dot_general(a.astype(BF16), b.astype(BF16), (((1,), (1,)), ((), ())),
                           preferred_element_type=F32)


def _dot_tn(a, b):
    return lax.dot_general(a.astype(BF16), b.astype(BF16), (((0,), (0,)), ((), ())),
                           preferred_element_type=F32)


def _params(*sem):
    return pltpu.CompilerParams(dimension_semantics=sem, vmem_limit_bytes=VMEM_LIMIT)


def _ada_kernel(c_ref, w_ref, b_ref, o_ref):
    s = _silu(c_ref[...])
    o_ref[...] = _dot(s, w_ref[...]) + b_ref[...]


def _ada(c_all, w, b, tn):
    m, d = c_all.shape
    n = w.shape[1]
    return pl.pallas_call(
        _ada_kernel,
        grid=(n // tn,),
        in_specs=[pl.BlockSpec((m, d), lambda j: (0, 0)),
                  pl.BlockSpec((d, tn), lambda j: (0, j)),
                  pl.BlockSpec((1, tn), lambda j: (0, j))],
        out_specs=pl.BlockSpec((m, tn), lambda j: (0, j)),
        out_shape=jax.ShapeDtypeStruct((m, n), F32),
        compiler_params=_params("arbitrary"),
        name="ada",
    )(c_all, w, b.reshape(1, n))


def _norm_mod(x, nw, shift, scale):
    y = x * lax.rsqrt(jnp.mean(x * x, axis=-1, keepdims=True) + EPS)
    y = y * nw
    return y * (1.0 + scale) + shift


def _proj_kernel(x_ref, nw_ref, sh_ref, sc_ref, w_ref, wg_ref, p_ref, pg_ref, h_ref):
    @pl.when(pl.program_id(1) == 0)
    def _():
        h = _norm_mod(x_ref[...], nw_ref[...], sh_ref[0], sc_ref[0])
        h_ref[...] = h.astype(BF16)
        pg_ref[...] = jnp.dot(h_ref[...], wg_ref[...], preferred_element_type=F32)

    p_ref[...] = jnp.dot(h_ref[...], w_ref[...], preferred_element_type=F32)


def _in_proj(x, nw, sh, sc, mod_map, w_main, w_gate, tm, tn):
    m, d = x.shape
    n = w_main.shape[1]
    r = sh.shape[1]
    return pl.pallas_call(
        _proj_kernel,
        grid=(m // tm, n // tn),
        in_specs=[pl.BlockSpec((tm, d), lambda i, j: (i, 0)),
                  pl.BlockSpec((1, d), lambda i, j: (0, 0)),
                  pl.BlockSpec((1, r, d), lambda i, j: (mod_map(i, 0), 0, 0)),
                  pl.BlockSpec((1, r, d), lambda i, j: (mod_map(i, 1), 0, 0)),
                  pl.BlockSpec((d, tn), lambda i, j: (0, j)),
                  pl.BlockSpec((d, LANES), lambda i, j: (0, 0))],
        out_specs=[pl.BlockSpec((tm, tn), lambda i, j: (i, j)),
                   pl.BlockSpec((tm, LANES), lambda i, j: (i, 0))],
        out_shape=[jax.ShapeDtypeStruct((m, n), F32),
                   jax.ShapeDtypeStruct((m, LANES), F32)],
        scratch_shapes=[pltpu.VMEM((tm, d), BF16)],
        compiler_params=_params("parallel", "arbitrary"),
        name="in_proj",
    )(x, nw, sh, sc, w_main, w_gate)


def _split(a):
    hi = a.astype(BF16)
    lo = (a - hi.astype(F32)).astype(BF16)
    return hi, lo


def _dot_split(a_parts, b_parts):
    ah, al = a_parts
    bh, bl = b_parts
    rows = ah.shape[0]
    top = jnp.dot(jnp.concatenate([ah, al], axis=0), bh, preferred_element_type=F32)
    return top[:rows] + top[rows:] + jnp.dot(ah, bl, preferred_element_type=F32)


def _neumann_correction(ms):
    c = ms[0].shape[0]
    ns = [-m for m in ms]
    pws = ms
    k = 2
    while k < c:
        parts = [_split(pw) for pw in pws]
        pws = [_dot_split(pp, pp) for pp in parts]
        k *= 2
        pparts = [_split(pw) for pw in pws]
        ns = [n + pw + _dot_split(_split(n), pp) for n, pw, pp in zip(ns, pws, pparts)]
    return ns


def _mixer_kernel(lg_ref, p_ref, pg_ref, cw_ref, gp_ref, dnw_ref, inv_ref,
                  conv0_ref, dn0_ref, ret0_ref,
                  mix_ref, conv_ref, dn_ref, ret_ref, xe_ref,
                  *, chunk, dn_heads, ret_heads, pos0):
    c = chunk
    n = pl.program_id(1)
    conv_ch = dn_heads * (2 * DN_DK + DN_DV)
    tail = SUBLANES
    heads = range(dn_heads)

    @pl.when(n == 0)
    def _():
        xe_ref[0:tail, :] = jnp.zeros((tail, conv_ch), F32)
        xe_ref[tail - (CONV_W - 1):tail, :] = conv0_ref[0]
        dn_ref[...] = dn0_ref[...]
        ret_ref[...] = ret0_ref[...]

    @pl.when(n > 0)
    def _():
        xe_ref[0:tail, :] = xe_ref[c:c + tail, :]

    xe_ref[tail:tail + c, :] = p_ref[:, 0:conv_ch]

    @pl.when(n == pl.num_programs(1) - 1)
    def _():
        conv_ref[0] = xe_ref[tail + c - (CONV_W - 1):tail + c, :]

    row = lax.broadcasted_iota(jnp.int32, (c, c), 0)
    col = lax.broadcasted_iota(jnp.int32, (c, c), 1)
    causal = row >= col
    strict = row > col

    pg = pg_ref[...]
    beta_all = jax.nn.sigmoid(pg)
    g_all = gp_ref[0:1, :] * jax.nn.softplus(pg + gp_ref[1:2, :])
    g_hi, g_lo = _split(g_all)
    g_lo2 = (g_all - g_hi.astype(F32) - g_lo.astype(F32)).astype(BF16)
    cs = jnp.dot(causal.astype(BF16), jnp.concatenate([g_hi, g_lo, g_lo2], axis=-1),
                 preferred_element_type=F32)
    decay_all = cs[:, :LANES] + cs[:, LANES:2 * LANES] + cs[:, 2 * LANES:]
    decay_t = decay_all.T

    def conv_silu(off):
        acc = xe_ref[pl.ds(tail, c), off:off + LANES] * cw_ref[CONV_W - 1:CONV_W, off:off + LANES]
        for i in range(1, CONV_W):
            acc = acc + (xe_ref[pl.ds(tail - i, c), off:off + LANES]
                         * cw_ref[CONV_W - 1 - i:CONV_W - i, off:off + LANES])
        return _silu(acc)

    ms, qkgs, rhss, qds, kds, cds = [], [], [], [], [], []
    for h in heads:
        q = conv_silu(h * DN_DK)
        k = conv_silu(dn_heads * DN_DK + h * DN_DK)
        v = conv_silu(2 * dn_heads * DN_DK + h * DN_DV)
        q = q * lax.rsqrt(jnp.sum(q * q, axis=-1, keepdims=True) + EPS) * (DN_DK ** -0.5)
        k = k * lax.rsqrt(jnp.sum(k * k, axis=-1, keepdims=True) + EPS)
        beta = beta_all[:, h:h + 1]
        dcol = decay_all[:, dn_heads + h:dn_heads + h + 1]
        drow = decay_t[dn_heads + h:dn_heads + h + 1, :]
        last = drow[:, c - 1:c]
        gam = jnp.where(causal, jnp.exp(jnp.where(causal, dcol - drow, 0.0)), 0.0)
        kb = k * beta
        both = _dot_nt(jnp.concatenate([kb, q], axis=0), k)
        ms.append(jnp.where(strict, both[:c] * gam, 0.0))
        qkgs.append((both[c:] * gam).astype(BF16))
        edec = jnp.exp(dcol)
        rhss.append(jnp.concatenate([v * beta, kb * edec], axis=-1))
        qds.append((q * edec).astype(BF16))
        kds.append(k * jnp.exp(last - dcol))
        cds.append(jnp.exp(last))

    ns = _neumann_correction(ms)
    sols = [r + _dot_split(_split(nn), _split(r)) for nn, r in zip(ns, rhss)]

    for h in heads:
        s = dn_ref[0, h]
        sb = s.astype(BF16)
        u_base = sols[h][:, :DN_DV]
        w_dec = sols[h][:, DN_DV:].astype(BF16)
        ws = jnp.dot(jnp.concatenate([w_dec, qds[h]], axis=0), sb, preferred_element_type=F32)
        u = (u_base - ws[:c]).astype(BF16)
        o = ws[c:] + jnp.dot(qkgs[h], u, preferred_element_type=F32)
        dn_ref[0, h] = s * cds[h] + _dot_tn(kds[h], u)
        o = o * lax.rsqrt(jnp.mean(o * o, axis=-1, keepdims=True) + EPS) * dnw_ref[...]
        zz = p_ref[:, conv_ch + h * DN_DV:conv_ch + (h + 1) * DN_DV]
        mix_ref[:, h * DN_DV:(h + 1) * DN_DV] = (o * _silu(zz)).astype(mix_ref.dtype)

    half = RET_DK // 2
    rq_off = conv_ch + dn_heads * DN_DV
    rk_off = rq_off + ret_heads * RET_DK
    rv_off = rk_off + ret_heads * RET_DK
    rg_off = rv_off + ret_heads * RET_DV
    mix_off = dn_heads * DN_DV
    tok = lax.broadcasted_iota(jnp.int32, (c, 1), 0)
    pos = (pos0 + n * c + tok).astype(F32)
    ang = pos * inv_ref[...]
    cos = jnp.cos(ang)
    sin = jnp.sin(ang)
    idx = tok.astype(F32)
    dpos = (row - col).astype(F32)

    def rot(off):
        x1 = p_ref[:, off:off + half]
        x2 = p_ref[:, off + half:off + 2 * half]
        return jnp.concatenate([x1 * cos - x2 * sin, x2 * cos + x1 * sin], axis=-1)

    inners, qxs, kzs, rvs = [], [], [], []
    for h in range(ret_heads):
        lg = lg_ref[h]
        rq = rot(rq_off + h * RET_DK)
        rk = rot(rk_off + h * RET_DK) * (RET_DK ** -0.5)
        rv = p_ref[:, rv_off + h * RET_DV:rv_off + (h + 1) * RET_DV].astype(BF16)
        dmask = jnp.where(causal, jnp.exp(jnp.where(causal, dpos, 0.0) * lg), 0.0)
        inners.append(jnp.dot((_dot_nt(rq, rk) * dmask).astype(BF16), rv,
                              preferred_element_type=F32))
        qxs.append((rq * jnp.exp((idx + 1.0) * lg)).astype(BF16))
        kzs.append(rk * jnp.exp((c - 1.0 - idx) * lg))
        rvs.append(rv)

    for h in range(ret_heads):
        s = ret_ref[0, h]
        o = inners[h] + jnp.dot(qxs[h], s.astype(BF16), preferred_element_type=F32)
        cd = jnp.exp(jnp.zeros((1, 1), F32) + c * lg_ref[h])
        ret_ref[0, h] = s * cd + _dot_tn(kzs[h], rvs[h])
        o = o * lax.rsqrt(jnp.mean(o * o, axis=-1, keepdims=True) + EPS)
        gg = p_ref[:, rg_off + h * RET_DV:rg_off + (h + 1) * RET_DV]
        mix_ref[:, mix_off + h * RET_DV:mix_off + (h + 1) * RET_DV] = (
            (o * _silu(gg)).astype(mix_ref.dtype))


def _mixer(p_main, p_gate, conv_w, gate_par, dn_norm, inv_freq, log_gamma,
           conv0, dn0, ret0, *, batch, chunk, pos0):
    m, width = p_main.shape
    nchunk = m // (batch * chunk)
    dn_heads = dn0.shape[1]
    ret_heads = ret0.shape[1]
    conv_ch = conv_w.shape[1]
    mix_w = dn_heads * DN_DV + ret_heads * RET_DV
    kern = functools.partial(_mixer_kernel, chunk=chunk, dn_heads=dn_heads,
                             ret_heads=ret_heads, pos0=pos0)
    full = lambda shape: pl.BlockSpec(shape, lambda b, n: (0,) * len(shape))
    return pl.pallas_call(
        kern,
        grid=(batch, nchunk),
        in_specs=[pl.BlockSpec(memory_space=pltpu.SMEM),
                  pl.BlockSpec((chunk, width), lambda b, n: (b * nchunk + n, 0)),
                  pl.BlockSpec((chunk, LANES), lambda b, n: (b * nchunk + n, 0)),
                  full(conv_w.shape), full(gate_par.shape), full(dn_norm.shape),
                  full(inv_freq.shape),
                  pl.BlockSpec((1,) + conv0.shape[1:], lambda b, n: (b, 0, 0)),
                  pl.BlockSpec((1,) + dn0.shape[1:], lambda b, n: (b, 0, 0, 0)),
                  pl.BlockSpec((1,) + ret0.shape[1:], lambda b, n: (b, 0, 0, 0))],
        out_specs=[pl.BlockSpec((chunk, mix_w), lambda b, n: (b * nchunk + n, 0)),
                   pl.BlockSpec((1,) + conv0.shape[1:], lambda b, n: (b, 0, 0)),
                   pl.BlockSpec((1,) + dn0.shape[1:], lambda b, n: (b, 0, 0, 0)),
                   pl.BlockSpec((1,) + ret0.shape[1:], lambda b, n: (b, 0, 0, 0))],
        out_shape=[jax.ShapeDtypeStruct((m, mix_w), BF16),
                   jax.ShapeDtypeStruct(conv0.shape, F32),
                   jax.ShapeDtypeStruct(dn0.shape, F32),
                   jax.ShapeDtypeStruct(ret0.shape, F32)],
        scratch_shapes=[pltpu.VMEM((chunk + SUBLANES, conv_ch), F32)],
        compiler_params=_params("parallel", "arbitrary"),
        name="mixer",
    )(log_gamma, p_main, p_gate, conv_w, gate_par, dn_norm, inv_freq, conv0, dn0, ret0)


def _out_kernel(x_ref, mix_ref, gm_ref, w_ref, o_ref):
    o_ref[...] = x_ref[...] + gm_ref[0] * jnp.dot(mix_ref[...], w_ref[...],
                                                  preferred_element_type=F32)


def _out_proj(x, mixed, gm, mod_map, w_out, tm):
    m, d = x.shape
    k = mixed.shape[1]
    r = gm.shape[1]
    return pl.pallas_call(
        _out_kernel,
        grid=(m // tm,),
        in_specs=[pl.BlockSpec((tm, d), lambda i: (i, 0)),
                  pl.BlockSpec((tm, k), lambda i: (i, 0)),
                  pl.BlockSpec((1, r, d), lambda i: (mod_map(i, 2), 0, 0)),
                  pl.BlockSpec((k, d), lambda i: (0, 0))],
        out_specs=pl.BlockSpec((tm, d), lambda i: (i, 0)),
        out_shape=jax.ShapeDtypeStruct((m, d), F32),
        compiler_params=_params("parallel"),
        name="out_proj",
    )(x, mixed, gm, w_out)


def _ffn_kernel(x_ref, nw_ref, sh_ref, sc_ref, gf_ref, wg_ref, wu_ref, wd_ref,
                nf_ref, shf_ref, scf_ref, y_ref, h_ref):
    j = pl.program_id(1)

    @pl.when(j == 0)
    def _():
        h = _norm_mod(x_ref[...], nw_ref[...], sh_ref[0], sc_ref[0])
        h_ref[...] = h.astype(BF16)
        y_ref[...] = jnp.zeros_like(y_ref)

    hb = h_ref[...]
    gate = jnp.dot(hb, wg_ref[...], preferred_element_type=F32)
    up = jnp.dot(hb, wu_ref[...], preferred_element_type=F32)
    act = (_silu(gate) * up).astype(BF16)
    y_ref[...] += jnp.dot(act, wd_ref[...], preferred_element_type=F32)

    @pl.when(j == pl.num_programs(1) - 1)
    def _():
        x2 = x_ref[...] + gf_ref[0] * y_ref[...]
        y_ref[...] = _norm_mod(x2, nf_ref[...], shf_ref[0], scf_ref[0])


def _ffn(x, nw, ada, mod_map, w_gu, w_down, nf, ada_fin, fin_map, tm, tf):
    m, d = x.shape
    dff = w_down.shape[0]
    nf_blocks = dff // tf
    r = ada.shape[1]
    mod = lambda k: pl.BlockSpec((1, r, d), lambda i, j: (mod_map(i, k), 0, 0))
    fin = lambda k: pl.BlockSpec((1, r, d), lambda i, j: (fin_map(i, k), 0, 0))
    return pl.pallas_call(
        _ffn_kernel,
        grid=(m // tm, nf_blocks),
        in_specs=[pl.BlockSpec((tm, d), lambda i, j: (i, 0)),
                  pl.BlockSpec((1, d), lambda i, j: (0, 0)),
                  mod(3), mod(4), mod(5),
                  pl.BlockSpec((d, tf), lambda i, j: (0, j)),
                  pl.BlockSpec((d, tf), lambda i, j: (0, j + nf_blocks)),
                  pl.BlockSpec((tf, d), lambda i, j: (j, 0)),
                  pl.BlockSpec((1, d), lambda i, j: (0, 0)),
                  fin(0), fin(1)],
        out_specs=pl.BlockSpec((tm, d), lambda i, j: (i, 0)),
        out_shape=jax.ShapeDtypeStruct((m, d), F32),
        scratch_shapes=[pltpu.VMEM((tm, d), BF16)],
        compiler_params=_params("parallel", "arbitrary"),
        name="ffn",
    )(x, nw, ada, ada, ada, w_gu, w_gu, w_down, nf, ada_fin, ada_fin)


def _ret_perm(ret_heads):
    per_head = np.concatenate([np.arange(0, RET_DK, 2), np.arange(1, RET_DK, 2)])
    return np.concatenate([h * RET_DK + per_head for h in range(ret_heads)])


def _stream(x, ada, ada_fin, conv0, dn0, ret0, weights, *, chunk, pos0, tm, tf, per_row):
    (nw_mix, nw_ffn, w_main, w_gate, conv_w, gate_par, dn_norm, inv_freq, log_gamma,
     w_out, w_gu, w_down, nf) = weights
    b, t, d = x.shape
    m = b * t
    xf = x.reshape(m, d)
    if per_row:
        ada_r = jnp.repeat(ada.reshape(b, N_ADA, d), t, axis=0).transpose(1, 0, 2)
        fin_r = jnp.repeat(ada_fin.reshape(b, 2, d), t, axis=0).transpose(1, 0, 2)
        mod_map = lambda i, k: k
        fin_map = lambda i, k: k
    else:
        tiles = t // tm
        ada_r = ada.reshape(b * N_ADA, 1, d)
        fin_r = ada_fin.reshape(b * 2, 1, d)
        mod_map = lambda i, k: (i // tiles) * N_ADA + k
        fin_map = lambda i, k: (i // tiles) * 2 + k

    p_main, p_gate = _in_proj(xf, nw_mix, ada_r, ada_r, mod_map, w_main, w_gate, tm, 1024)
    mixed, conv_n, dn_n, ret_n = _mixer(p_main, p_gate, conv_w, gate_par, dn_norm, inv_freq,
                                        log_gamma, conv0, dn0, ret0,
                                        batch=b, chunk=chunk, pos0=pos0)
    x1 = _out_proj(xf, mixed, ada_r, mod_map, w_out, tm)
    y = _ffn(x1, nw_ffn, ada_r, mod_map, w_gu, w_down, nf, fin_r, fin_map, tm, tf)
    return y.reshape(b, t, d), conv_n, dn_n, ret_n


def kernel(x_prompt, x_sample, state_conv, state_delta, state_ret, c_prompt, c_sample, norm_mix, norm_ffn, w_ada, b_ada, w_in, conv_w, dn_a_log, dn_dt_bias, dn_norm, w_out, w_gu, w_down, norm_final, w_ada_final, b_ada_final):
    bp, tp, d = x_prompt.shape
    bs, ts, _ = x_sample.shape
    depth = w_in.shape[0]
    assert depth == 1, "single-layer pipeline"
    dn_heads = state_delta.shape[2]
    ret_heads = state_ret.shape[2]
    conv_ch = conv_w.shape[2]
    assert 2 * dn_heads <= LANES

    nb = bp + bs
    pad = (-nb) % (2 * SUBLANES)
    c_all = jnp.concatenate([c_prompt, c_sample, jnp.zeros((pad, d), F32)], axis=0)
    ada_all = _ada(c_all, w_ada[0], b_ada[0], 1024)
    fin_all = _ada(c_all, w_ada_final, b_ada_final, 1024)

    l = 0
    g_off = conv_ch + dn_heads * DN_DV
    r_off = g_off + 2 * dn_heads
    perm = _ret_perm(ret_heads)
    w_l = w_in[l]
    rq_w = w_l[:, r_off:r_off + ret_heads * RET_DK][:, perm]
    rk_w = w_l[:, r_off + ret_heads * RET_DK:r_off + 2 * ret_heads * RET_DK][:, perm]
    w_main = jnp.concatenate([w_l[:, :g_off], rq_w, rk_w,
                              w_l[:, r_off + 2 * ret_heads * RET_DK:]], axis=1).astype(BF16)
    w_gate = jnp.concatenate([w_l[:, g_off:r_off],
                              jnp.zeros((d, LANES - 2 * dn_heads), F32)], axis=1).astype(BF16)
    lane_pad = jnp.zeros((LANES - 2 * dn_heads,), F32)
    gate_par = jnp.stack([
        jnp.concatenate([jnp.zeros((dn_heads,), F32), -jnp.exp(dn_a_log[l].astype(F32)), lane_pad]),
        jnp.concatenate([jnp.zeros((dn_heads,), F32), dn_dt_bias[l].astype(F32), lane_pad])])
    inv_freq = (1.0 / (ROPE_BASE ** jnp.linspace(0.0, 1.0, RET_DK // 2, dtype=F32))).reshape(1, -1)
    log_gamma = jnp.log(1.0 - 2.0 ** (-5.0 - jnp.arange(ret_heads, dtype=F32)))
    weights = (norm_mix[l].reshape(1, d), norm_ffn[l].reshape(1, d), w_main, w_gate, conv_w[l],
               gate_par, dn_norm[l].reshape(1, -1), inv_freq, log_gamma,
               w_out[l].astype(BF16), w_gu[l].astype(BF16), w_down[l].astype(BF16),
               norm_final.reshape(1, d))

    inv_perm = np.argsort(perm[:RET_DK])
    zc = jnp.zeros((bp, CONV_W - 1, conv_ch), F32)
    zd = jnp.zeros((bp,) + state_delta.shape[2:], F32)
    zr = jnp.zeros((bp,) + state_ret.shape[2:], F32)

    y_p, cp, dp, rp = _stream(x_prompt, ada_all[:bp], fin_all[:bp], zc, zd, zr, weights,
                              chunk=CHUNK, pos0=0, tm=512, tf=512, per_row=False)
    rs_in = state_ret[l][:, :, perm[:RET_DK], :]
    y_s, cs, ds, rs = _stream(x_sample, ada_all[bp:nb], fin_all[bp:nb], state_conv[l],
                              state_delta[l], rs_in, weights,
                              chunk=ts, pos0=PAST_LEN, tm=bs * ts, tf=512, per_row=True)
    rp = rp[:, :, inv_perm, :]
    rs = rs[:, :, inv_perm, :]
    return (y_p, y_s, cp[None], dp[None], rp[None], cs[None], ds[None], rs[None])
```

```python
import functools
from typing import NamedTuple

import numpy as np
import jax
import jax.numpy as jnp
from jax import lax
from jax.experimental import pallas as pl
from jax.experimental.pallas import tpu as pltpu

F32 = jnp.float32
BF16 = jnp.bfloat16

DN_DK = 128
DN_DV = 128
RET_DK = 256
RET_DV = 256
CONV_W = 4
N_ADA = 6
CHUNK = 64
PAST_LEN = 2048
ROPE_BASE = 10000.0
EPS = 1e-6

LANES = 128
SUBLANES = 8
VMEM_LIMIT = 56 * 1024 * 1024


def _silu(x):
    return x * jax.nn.sigmoid(x)


def _dot(a, b):
    return jnp.dot(a.astype(BF16), b.astype(BF16), preferred_element_type=F32)


def _dot_nt(a, b):
    return lax.dot_general(a.astype(BF16), b.astype(BF16), (((1,), (1,)), ((), ())),
                           preferred_element_type=F32)


def _dot_tn(a, b):
    return lax.dot_general(a.astype(BF16), b.astype(BF16), (((0,), (0,)), ((), ())),
                           preferred_element_type=F32)


def _params(*sem):
    return pltpu.CompilerParams(dimension_semantics=sem, vmem_limit_bytes=VMEM_LIMIT)


def _ada_kernel(c_ref, w_ref, b_ref, o_ref):
    s = _silu(c_ref[...])
    o_ref[...] = _dot(s, w_ref[...]) + b_ref[...]


def _ada(c_all, w, b, tn):
    m, d = c_all.shape
    n = w.shape[1]
    return pl.pallas_call(
        _ada_kernel,
        grid=(n // tn,),
        in_specs=[pl.BlockSpec((m, d), lambda j: (0, 0)),
                  pl.BlockSpec((d, tn), lambda j: (0, j)),
                  pl.BlockSpec((1, tn), lambda j: (0, j))],
        out_specs=pl.BlockSpec((m, tn), lambda j: (0, j)),
        out_shape=jax.ShapeDtypeStruct((m, n), F32),
        compiler_params=_params("arbitrary"),
        name="ada",
    )(c_all, w, b.reshape(1, n))


def _norm_mod(x, nw, shift, scale):
    y = x * lax.rsqrt(jnp.mean(x * x, axis=-1, keepdims=True) + EPS)
    y = y * nw
    return y * (1.0 + scale) + shift


def _proj_kernel(x_ref, nw_ref, sh_ref, sc_ref, w_ref, wg_ref, p_ref, pg_ref, h_ref):
    @pl.when(pl.program_id(1) == 0)
    def _():
        h = _norm_mod(x_ref[...], nw_ref[...], sh_ref[0], sc_ref[0])
        h_ref[...] = h.astype(BF16)
        pg_ref[...] = jnp.dot(h_ref[...], wg_ref[...], preferred_element_type=F32)

    p_ref[...] = jnp.dot(h_ref[...], w_ref[...], preferred_element_type=F32)


def _in_proj(x, nw, sh, sc, mod_map, w_main, w_gate, tm, tn):
    m, d = x.shape
    n = w_main.shape[1]
    r = sh.shape[1]
    return pl.pallas_call(
        _proj_kernel,
        grid=(m // tm, n // tn),
        in_specs=[pl.BlockSpec((tm, d), lambda i, j: (i, 0)),
                  pl.BlockSpec((1, d), lambda i, j: (0, 0)),
                  pl.BlockSpec((1, r, d), lambda i, j: (mod_map(i, 0), 0, 0)),
                  pl.BlockSpec((1, r, d), lambda i, j: (mod_map(i, 1), 0, 0)),
                  pl.BlockSpec((d, tn), lambda i, j: (0, j)),
                  pl.BlockSpec((d, LANES), lambda i, j: (0, 0))],
        out_specs=[pl.BlockSpec((tm, tn), lambda i, j: (i, j)),
                   pl.BlockSpec((tm, LANES), lambda i, j: (i, 0))],
        out_shape=[jax.ShapeDtypeStruct((m, n), F32),
                   jax.ShapeDtypeStruct((m, LANES), F32)],
        scratch_shapes=[pltpu.VMEM((tm, d), BF16)],
        compiler_params=_params("parallel", "arbitrary"),
        name="in_proj",
    )(x, nw, sh, sc, w_main, w_gate)


def _split(a):
    hi = a.astype(BF16)
    lo = (a - hi.astype(F32)).astype(BF16)
    return hi, lo


def _dot_split(a_parts, b_parts):
    ah, al = a_parts
    bh, bl = b_parts
    rows = ah.shape[0]
    top = jnp.dot(jnp.concatenate([ah, al], axis=0), bh, preferred_element_type=F32)
    return top[:rows] + top[rows:] + jnp.dot(ah, bl, preferred_element_type=F32)


def _neumann_correction(ms, fillers=()):
    c = ms[0].shape[0]
    fillers = list(fillers)

    def fill():
        if fillers:
            fillers.pop(0)()

    ns = [-m for m in ms]
    pws = [_dot_split(_split(m), _split(m)) for m in ms]
    fill()
    k = 2
    while k < c:
        pps = [_split(pw) for pw in pws]
        nps = [_split(n) for n in ns]
        if 2 * k < c:
            tops = [_dot_split((jnp.concatenate([pp[0], npart[0]], axis=0),
                                jnp.concatenate([pp[1], npart[1]], axis=0)), pp)
                    for pp, npart in zip(pps, nps)]
            fill()
            ns = [n + pw + top[c:] for n, pw, top in zip(ns, pws, tops)]
            pws = [top[:c] for top in tops]
        else:
            prods = [_dot_split(npart, pp) for npart, pp in zip(nps, pps)]
            fill()
            ns = [n + pw + prod for n, pw, prod in zip(ns, pws, prods)]
        k *= 2
    while fillers:
        fill()
    return ns


def _mixer_kernel(lg_ref, pq_ref, pk_ref, pv_ref, pg_ref, pz_ref, prq_ref, prk_ref, prv_ref, prg_ref,
                  cw_ref, gp_ref, dnw_ref, inv_ref, conv0_ref, dn0_ref, ret0_ref,
                  mix_ref, conv_ref, dn_ref, ret_ref,
                  xe_ref, m_s, qkg_s, rhs_s, qd_s, kd_s, cd_s,
                  *, chunk, nb, dn_heads, ret_heads, pos0):
    c = chunk
    n = pl.program_id(1)
    nsteps = pl.num_programs(1)
    tail = SUBLANES
    batches = range(nb)
    units = [(b, h) for b in batches for h in range(dn_heads)]
    runits = [(b, h) for b in batches for h in range(ret_heads)]
    uid = lambda u: u[0] * dn_heads + u[1]
    qk_w = dn_heads * DN_DK
    wr = n % 2
    rd = 1 - wr

    @pl.when(n == 0)
    def _():
        m_s[...] = jnp.zeros_like(m_s)
        qkg_s[...] = jnp.zeros_like(qkg_s)
        rhs_s[...] = jnp.zeros_like(rhs_s)
        qd_s[...] = jnp.zeros_like(qd_s)
        kd_s[...] = jnp.zeros_like(kd_s)
        cd_s[...] = jnp.zeros_like(cd_s)
        dn_ref[...] = dn0_ref[...]
        ret_ref[...] = ret0_ref[...]
        for b in batches:
            xe_ref[b, 0:tail, :] = jnp.zeros((tail, xe_ref.shape[2]), F32)
            xe_ref[b, tail - (CONV_W - 1):tail, :] = conv0_ref[b]

    row = lax.broadcasted_iota(jnp.int32, (c, c), 0)
    col = lax.broadcasted_iota(jnp.int32, (c, c), 1)
    causal = row >= col
    strict = row > col

    half = RET_DK // 2
    mix_off = dn_heads * DN_DV

    beta_all, cs = [], []
    for b in batches:
        xe_ref[b, tail:tail + c, 0:qk_w] = pq_ref[b]
        xe_ref[b, tail:tail + c, qk_w:2 * qk_w] = pk_ref[b]
        xe_ref[b, tail:tail + c, 2 * qk_w:] = pv_ref[b]
        pg = pg_ref[b]
        beta_all.append(jax.nn.sigmoid(pg))
        g_all = gp_ref[0:1, :] * jax.nn.softplus(pg + gp_ref[1:2, :])
        g_hi, g_lo = _split(g_all)
        g_lo2 = (g_all - g_hi.astype(F32) - g_lo.astype(F32)).astype(BF16)
        cs.append(jnp.dot(causal.astype(BF16), jnp.concatenate([g_hi, g_lo, g_lo2], axis=-1),
                          preferred_element_type=F32))

    def conv_silu(b, off):
        acc = xe_ref[b, pl.ds(tail, c), off:off + LANES] * cw_ref[CONV_W - 1:CONV_W, off:off + LANES]
        for i in range(1, CONV_W):
            acc = acc + (xe_ref[b, pl.ds(tail - i, c), off:off + LANES]
                         * cw_ref[CONV_W - 1 - i:CONV_W - i, off:off + LANES])
        return _silu(acc)

    prep = {}

    def prep_values():
        for b in batches:
            decay_all = cs[b][:, :LANES] + cs[b][:, LANES:2 * LANES] + cs[b][:, 2 * LANES:]
            decay_t = decay_all.T
            for h in range(dn_heads):
                i = uid((b, h))
                q = conv_silu(b, h * DN_DK)
                k = conv_silu(b, qk_w + h * DN_DK)
                v = conv_silu(b, 2 * qk_w + h * DN_DV)
                q = q * lax.rsqrt(jnp.sum(q * q, axis=-1, keepdims=True) + EPS) * (DN_DK ** -0.5)
                k = k * lax.rsqrt(jnp.sum(k * k, axis=-1, keepdims=True) + EPS)
                beta = beta_all[b][:, h:h + 1]
                dcol = decay_all[:, dn_heads + h:dn_heads + h + 1]
                drow = decay_t[dn_heads + h:dn_heads + h + 1, :]
                last = drow[:, c - 1:c]
                kb = k * beta
                edec = jnp.exp(dcol)
                prep[i] = dict(k=k, lhs=jnp.concatenate([kb, q], axis=0).astype(BF16),
                               gam=jnp.where(causal, jnp.exp(jnp.where(causal, dcol - drow, 0.0)), 0.0))
                rhs_s[wr, i] = jnp.concatenate([v * beta, kb * edec], axis=-1)
                qd_s[wr, i] = (q * edec).astype(BF16)
                kd_s[wr, i] = (k * jnp.exp(last - dcol)).astype(BF16)
                cd_s[wr, i] = jnp.broadcast_to(jnp.exp(last), (1, LANES))

    def prep_matmuls():
        for u in units:
            p = prep[uid(u)]
            p["both"] = _dot_nt(p["lhs"], p["k"])

    def prep_store():
        for u in units:
            i = uid(u)
            both, gam = prep[i]["both"], prep[i]["gam"]
            m_s[wr, i] = jnp.where(strict, both[:c] * gam, 0.0)
            qkg_s[wr, i] = (both[c:] * gam).astype(BF16)

    tok = lax.broadcasted_iota(jnp.int32, (c, 1), 0)
    idx = tok.astype(F32)
    ret = {}

    def rot(ref, b, off, cos, sin):
        x1 = ref[b, :, off:off + half]
        x2 = ref[b, :, off + half:off + 2 * half]
        return jnp.concatenate([x1 * cos - x2 * sin, x2 * cos + x1 * sin], axis=-1)

    def ret_values():
        pos = (pos0 + (n - 1) * c + tok).astype(F32)
        ang = pos * inv_ref[...]
        cos = jnp.cos(ang)
        sin = jnp.sin(ang)
        for u in runits:
            b, h = u
            lg = lg_ref[h]
            rq = rot(prq_ref, b, h * RET_DK, cos, sin)
            rk = rot(prk_ref, b, h * RET_DK, cos, sin) * (RET_DK ** -0.5)
            ret[u] = dict(rq=rq.astype(BF16), rk=rk.astype(BF16),
                          rv=prv_ref[b, :, h * RET_DV:(h + 1) * RET_DV].astype(BF16),
                          qx=(rq * jnp.exp((idx + 1.0) * lg)).astype(BF16),
                          kz=(rk * jnp.exp((c - 1.0 - idx) * lg)).astype(BF16))

    def ret_scores():
        for u in runits:
            ret[u]["qk"] = _dot_nt(ret[u]["rq"], ret[u]["rk"])

    def ret_inner():
        dpos = (row - col).astype(F32)
        dmask = [jnp.where(causal, jnp.exp(jnp.where(causal, dpos, 0.0) * lg_ref[h]), 0.0)
                 for h in range(ret_heads)]
        for u in runits:
            ret[u]["inner"] = jnp.dot((ret[u]["qk"] * dmask[u[1]]).astype(BF16), ret[u]["rv"],
                                      preferred_element_type=F32)

    ns = _neumann_correction([m_s[rd, uid(u)] for u in units],
                             [prep_values, prep_matmuls, prep_store, ret_values, ret_scores, ret_inner])
    rhs = [rhs_s[rd, uid(u)] for u in units]
    sols = [r + _dot_split(_split(nn), _split(r)) for r, nn in zip(rhs, ns)]

    sd = [dn_ref[b, h] for b, h in units]
    ws = [jnp.dot(jnp.concatenate([sol[:, DN_DV:].astype(BF16), qd_s[rd, uid(u)]], axis=0),
                  s.astype(BF16), preferred_element_type=F32) for u, sol, s in zip(units, sols, sd)]
    sr = [ret_ref[b, h] for b, h in runits]
    ro = [ret[u]["inner"] + jnp.dot(ret[u]["qx"], s.astype(BF16), preferred_element_type=F32)
          for u, s in zip(runits, sr)]
    rupd = [_dot_tn(ret[u]["kz"], ret[u]["rv"]) for u in runits]
    us = [(sol[:, :DN_DV] - w[:c]).astype(BF16) for sol, w in zip(sols, ws)]
    o2 = [jnp.dot(qkg_s[rd, uid(u)], uu, preferred_element_type=F32) for u, uu in zip(units, us)]
    upd = [_dot_tn(kd_s[rd, uid(u)], uu) for u, uu in zip(units, us)]
    cdr = [jnp.exp(jnp.zeros((1, 1), F32) + c * lg_ref[h]) for h in range(ret_heads)]
    for k, (b, h) in enumerate(runits):
        ret_ref[b, h] = sr[k] * cdr[h] + rupd[k]
        o = ro[k]
        o = o * lax.rsqrt(jnp.mean(o * o, axis=-1, keepdims=True) + EPS)
        gg = prg_ref[b, :, h * RET_DV:(h + 1) * RET_DV]
        mix_ref[b, :, mix_off + h * RET_DV:mix_off + (h + 1) * RET_DV] = (
            (o * _silu(gg)).astype(mix_ref.dtype))
    for k, (b, h) in enumerate(units):
        dn_ref[b, h] = sd[k] * cd_s[rd, uid((b, h))][:, 0:1] + upd[k]
        o = ws[k][c:] + o2[k]
        o = o * lax.rsqrt(jnp.mean(o * o, axis=-1, keepdims=True) + EPS) * dnw_ref[...]
        zz = pz_ref[b, :, h * DN_DV:(h + 1) * DN_DV]
        mix_ref[b, :, h * DN_DV:(h + 1) * DN_DV] = (o * _silu(zz)).astype(mix_ref.dtype)

    for b in batches:
        xe_ref[b, 0:tail, :] = xe_ref[b, c:c + tail, :]

    @pl.when(n == 0)
    def _():
        dn_ref[...] = dn0_ref[...]
        ret_ref[...] = ret0_ref[...]

    @pl.when(n == nsteps - 2)
    def _():
        for b in batches:
            conv_ref[b] = xe_ref[b, tail + c - (CONV_W - 1):tail + c, :]


def _mixer(p_main, p_gate, conv_w, gate_par, dn_norm, inv_freq, log_gamma,
           conv0, dn0, ret0, *, batch, chunk, pos0):
    m, width = p_main.shape
    t = m // batch
    nchunk = t // chunk
    dn_heads = dn0.shape[1]
    ret_heads = ret0.shape[1]
    conv_ch = conv_w.shape[1]
    mix_w = dn_heads * DN_DV + ret_heads * RET_DV
    colw = dn_heads * DN_DK
    assert width == 8 * colw and ret_heads * RET_DK == colw and dn_heads * DN_DV == colw
    nb = 2 if batch % 2 == 0 else 1
    kern = functools.partial(_mixer_kernel, chunk=chunk, nb=nb, dn_heads=dn_heads,
                             ret_heads=ret_heads, pos0=pos0)
    full = lambda shape: pl.BlockSpec(shape, lambda g, n: (0,) * len(shape))
    cur = lambda n: jnp.minimum(n, nchunk - 1)
    prev = lambda n: jnp.maximum(n - 1, 0)
    cur_col = lambda j: pl.BlockSpec((nb, chunk, colw), lambda g, n: (g, cur(n), j))
    prev_col = lambda j: pl.BlockSpec((nb, chunk, colw), lambda g, n: (g, prev(n), j))
    state = lambda a: pl.BlockSpec((nb,) + a.shape[1:], lambda g, n: (g,) + (0,) * (a.ndim - 1))
    p3 = p_main.reshape(batch, t, width)
    units = nb * dn_heads
    mixed, conv_n, dn_n, ret_n = pl.pallas_call(
        kern,
        grid=(batch // nb, nchunk + 1),
        in_specs=[pl.BlockSpec(memory_space=pltpu.SMEM),
                  cur_col(0), cur_col(1), cur_col(2),
                  pl.BlockSpec((nb, chunk, LANES), lambda g, n: (g, cur(n), 0)),
                  prev_col(3), prev_col(4), prev_col(5), prev_col(6), prev_col(7),
                  full(conv_w.shape), full(gate_par.shape), full(dn_norm.shape),
                  full(inv_freq.shape), state(conv0), state(dn0), state(ret0)],
        out_specs=[pl.BlockSpec((nb, chunk, mix_w), lambda g, n: (g, prev(n), 0)),
                   state(conv0), state(dn0), state(ret0)],
        out_shape=[jax.ShapeDtypeStruct((batch, t, mix_w), BF16),
                   jax.ShapeDtypeStruct(conv0.shape, F32),
                   jax.ShapeDtypeStruct(dn0.shape, F32),
                   jax.ShapeDtypeStruct(ret0.shape, F32)],
        scratch_shapes=[pltpu.VMEM((nb, chunk + SUBLANES, conv_ch), F32),
                        pltpu.VMEM((2, units, chunk, chunk), F32),
                        pltpu.VMEM((2, units, chunk, chunk), BF16),
                        pltpu.VMEM((2, units, chunk, DN_DV + DN_DK), F32),
                        pltpu.VMEM((2, units, chunk, DN_DK), BF16),
                        pltpu.VMEM((2, units, chunk, DN_DK), BF16),
                        pltpu.VMEM((2, units, 1, LANES), F32)],
        compiler_params=_params("parallel", "arbitrary"),
        name="mixer",
    )(log_gamma, p3, p3, p3, p_gate.reshape(batch, t, LANES), p3, p3, p3, p3, p3,
      conv_w, gate_par, dn_norm, inv_freq, conv0, dn0, ret0)
    return mixed.reshape(m, mix_w), conv_n, dn_n, ret_n


def _ffn_kernel(x_ref, mix_ref, gm_ref, wo_ref, nw_ref, sh_ref, sc_ref, gf_ref, wg_ref, wu_ref, wd_ref,
                nf_ref, shf_ref, scf_ref, y_ref, x1_ref, h_ref):
    j = pl.program_id(1)

    @pl.when(j == 0)
    def _():
        x1 = x_ref[...] + gm_ref[0] * jnp.dot(mix_ref[...], wo_ref[...], preferred_element_type=F32)
        x1_ref[...] = x1
        h = _norm_mod(x1, nw_ref[...], sh_ref[0], sc_ref[0])
        h_ref[...] = h.astype(BF16)
        y_ref[...] = jnp.zeros_like(y_ref)

    hb = h_ref[...]
    gate = jnp.dot(hb, wg_ref[...], preferred_element_type=F32)
    up = jnp.dot(hb, wu_ref[...], preferred_element_type=F32)
    act = (_silu(gate) * up).astype(BF16)
    y_ref[...] += jnp.dot(act, wd_ref[...], preferred_element_type=F32)

    @pl.when(j == pl.num_programs(1) - 1)
    def _():
        x2 = x1_ref[...] + gf_ref[0] * y_ref[...]
        y_ref[...] = _norm_mod(x2, nf_ref[...], shf_ref[0], scf_ref[0])


def _ffn(x, mixed, w_out, nw, ada, mod_map, w_gu, w_down, nf, ada_fin, fin_map, tm, tf):
    m, d = x.shape
    kmix = mixed.shape[1]
    dff = w_down.shape[0]
    nf_blocks = dff // tf
    r = ada.shape[1]
    mod = lambda k: pl.BlockSpec((1, r, d), lambda i, j: (mod_map(i, k), 0, 0))
    fin = lambda k: pl.BlockSpec((1, r, d), lambda i, j: (fin_map(i, k), 0, 0))
    return pl.pallas_call(
        _ffn_kernel,
        grid=(m // tm, nf_blocks),
        in_specs=[pl.BlockSpec((tm, d), lambda i, j: (i, 0)),
                  pl.BlockSpec((tm, kmix), lambda i, j: (i, 0)),
                  mod(2),
                  pl.BlockSpec((kmix, d), lambda i, j: (0, 0), pipeline_mode=pl.Buffered(1)),
                  pl.BlockSpec((1, d), lambda i, j: (0, 0)),
                  mod(3), mod(4), mod(5),
                  pl.BlockSpec((d, tf), lambda i, j: (0, j)),
                  pl.BlockSpec((d, tf), lambda i, j: (0, j + nf_blocks)),
                  pl.BlockSpec((tf, d), lambda i, j: (j, 0)),
                  pl.BlockSpec((1, d), lambda i, j: (0, 0)),
                  fin(0), fin(1)],
        out_specs=pl.BlockSpec((tm, d), lambda i, j: (i, 0)),
        out_shape=jax.ShapeDtypeStruct((m, d), F32),
        scratch_shapes=[pltpu.VMEM((tm, d), F32), pltpu.VMEM((tm, d), BF16)],
        compiler_params=_params("parallel", "arbitrary"),
        name="ffn",
    )(x, mixed, ada, w_out, nw, ada, ada, ada, w_gu, w_gu, w_down, nf, ada_fin, ada_fin)


def _ret_perm(ret_heads):
    per_head = np.concatenate([np.arange(0, RET_DK, 2), np.arange(1, RET_DK, 2)])
    return np.concatenate([h * RET_DK + per_head for h in range(ret_heads)])


class _Tiles(NamedTuple):
    proj_rows: int
    proj_cols: int
    ffn_rows: int
    ffn_cols: int


def _stream(x, ada, ada_fin, conv0, dn0, ret0, weights, *, chunk, pos0, tiles, per_row):
    (nw_mix, nw_ffn, w_main, w_gate, conv_w, gate_par, dn_norm, inv_freq, log_gamma,
     w_out, w_gu, w_down, nf) = weights
    b, t, d = x.shape
    m = b * t
    xf = x.reshape(m, d)
    if per_row:
        ada_r = jnp.repeat(ada.reshape(b, N_ADA, d), t, axis=0).transpose(1, 0, 2)
        fin_r = jnp.repeat(ada_fin.reshape(b, 2, d), t, axis=0).transpose(1, 0, 2)
        mod_map = lambda tm: (lambda i, k: k)
        fin_map = lambda tm: (lambda i, k: k)
    else:
        assert t % tiles.proj_rows == 0 and t % tiles.ffn_rows == 0
        ada_r = ada.reshape(b * N_ADA, 1, d)
        fin_r = ada_fin.reshape(b * 2, 1, d)
        mod_map = lambda tm: (lambda i, k: (i // (t // tm)) * N_ADA + k)
        fin_map = lambda tm: (lambda i, k: (i // (t // tm)) * 2 + k)

    p_main, p_gate = _in_proj(xf, nw_mix, ada_r, ada_r, mod_map(tiles.proj_rows), w_main, w_gate,
                              tiles.proj_rows, tiles.proj_cols)
    mixed, conv_n, dn_n, ret_n = _mixer(p_main, p_gate, conv_w, gate_par, dn_norm, inv_freq,
                                        log_gamma, conv0, dn0, ret0,
                                        batch=b, chunk=chunk, pos0=pos0)
    y = _ffn(xf, mixed, w_out, nw_ffn, ada_r, mod_map(tiles.ffn_rows), w_gu, w_down, nf, fin_r,
             fin_map(tiles.ffn_rows), tiles.ffn_rows, tiles.ffn_cols)
    return y.reshape(b, t, d), conv_n, dn_n, ret_n


def kernel(x_prompt, x_sample, state_conv, state_delta, state_ret, c_prompt, c_sample, norm_mix, norm_ffn, w_ada, b_ada, w_in, conv_w, dn_a_log, dn_dt_bias, dn_norm, w_out, w_gu, w_down, norm_final, w_ada_final, b_ada_final):
    bp, tp, d = x_prompt.shape
    bs, ts, _ = x_sample.shape
    depth = w_in.shape[0]
    assert depth == 1, "single-layer pipeline"
    dn_heads = state_delta.shape[2]
    ret_heads = state_ret.shape[2]
    conv_ch = conv_w.shape[2]
    assert 2 * dn_heads <= LANES

    nb = bp + bs
    pad = (-nb) % (2 * SUBLANES)
    c_all = jnp.concatenate([c_prompt, c_sample, jnp.zeros((pad, d), F32)], axis=0)
    ada_all = _ada(c_all, w_ada[0], b_ada[0], 1024)
    fin_all = _ada(c_all, w_ada_final, b_ada_final, 1024)

    l = 0
    g_off = conv_ch + dn_heads * DN_DV
    r_off = g_off + 2 * dn_heads
    perm = _ret_perm(ret_heads)
    w_l = w_in[l]
    rq_w = w_l[:, r_off:r_off + ret_heads * RET_DK][:, perm]
    rk_w = w_l[:, r_off + ret_heads * RET_DK:r_off + 2 * ret_heads * RET_DK][:, perm]
    w_main = jnp.concatenate([w_l[:, :g_off], rq_w, rk_w,
                              w_l[:, r_off + 2 * ret_heads * RET_DK:]], axis=1).astype(BF16)
    w_gate = jnp.concatenate([w_l[:, g_off:r_off],
                              jnp.zeros((d, LANES - 2 * dn_heads), F32)], axis=1).astype(BF16)
    lane_pad = jnp.zeros((LANES - 2 * dn_heads,), F32)
    gate_par = jnp.stack([
        jnp.concatenate([jnp.zeros((dn_heads,), F32), -jnp.exp(dn_a_log[l].astype(F32)), lane_pad]),
        jnp.concatenate([jnp.zeros((dn_heads,), F32), dn_dt_bias[l].astype(F32), lane_pad])])
    inv_freq = (1.0 / (ROPE_BASE ** jnp.linspace(0.0, 1.0, RET_DK // 2, dtype=F32))).reshape(1, -1)
    log_gamma = jnp.log(1.0 - 2.0 ** (-5.0 - jnp.arange(ret_heads, dtype=F32)))
    weights = (norm_mix[l].reshape(1, d), norm_ffn[l].reshape(1, d), w_main, w_gate, conv_w[l],
               gate_par, dn_norm[l].reshape(1, -1), inv_freq, log_gamma,
               w_out[l].astype(BF16), w_gu[l].astype(BF16), w_down[l].astype(BF16),
               norm_final.reshape(1, d))

    inv_perm = np.argsort(perm[:RET_DK])
    zc = jnp.zeros((bp, CONV_W - 1, conv_ch), F32)
    zd = jnp.zeros((bp,) + state_delta.shape[2:], F32)
    zr = jnp.zeros((bp,) + state_ret.shape[2:], F32)

    y_p, cp, dp, rp = _stream(x_prompt, ada_all[:bp], fin_all[:bp], zc, zd, zr, weights,
                              chunk=CHUNK, pos0=0, per_row=False,
                              tiles=_Tiles(proj_rows=1024, proj_cols=1024, ffn_rows=512, ffn_cols=512))
    rs_in = state_ret[l][:, :, perm[:RET_DK], :]
    y_s, cs, ds, rs = _stream(x_sample, ada_all[bp:nb], fin_all[bp:nb], state_conv[l],
                              state_delta[l], rs_in, weights,
                              chunk=ts, pos0=PAST_LEN, per_row=True,
                              tiles=_Tiles(proj_rows=bs * ts, proj_cols=1024, ffn_rows=bs * ts,
                                           ffn_cols=512))
    rp = rp[:, :, inv_perm, :]
    rs = rs[:, :, inv_perm, :]
    return (y_p, y_s, cp[None], dp[None], rp[None], cs[None], ds[None], rs[None])
```

```python
import functools
from typing import NamedTuple

import jax
import jax.numpy as jnp
from jax import lax
from jax.experimental import pallas as pl
from jax.experimental.pallas import tpu as pltpu

F32 = jnp.float32
BF16 = jnp.bfloat16

DN_DK = 128
DN_DV = 128
RET_DK = 256
RET_DV = 256
CONV_W = 4
N_ADA = 6
CHUNK = 64
PAST_LEN = 2048
ROPE_BASE = 10000.0
EPS = 1e-6

LANES = 128
SUBLANES = 8
VMEM_LIMIT = 56 * 1024 * 1024


def _silu(x):
    return x * jax.nn.sigmoid(x)


def _dot(a, b):
    return jnp.dot(a.astype(BF16), b.astype(BF16), preferred_element_type=F32)


def _dot_nt(a, b):
    return lax.dot_general(a.astype(BF16), b.astype(BF16), (((1,), (1,)), ((), ())),
                           preferred_element_type=F32)


def _dot_tn(a, b):
    return lax.dot_general(a.astype(BF16), b.astype(BF16), (((0,), (0,)), ((), ())),
                           preferred_element_type=F32)


def _params(*sem):
    return pltpu.CompilerParams(dimension_semantics=sem, vmem_limit_bytes=VMEM_LIMIT)


def _ada_kernel(c_ref, w_ref, b_ref, o_ref):
    s = _silu(c_ref[...])
    o_ref[...] = _dot(s, w_ref[...]) + b_ref[...]


def _ada(c_all, w, b, tn):
    m, d = c_all.shape
    n = w.shape[1]
    return pl.pallas_call(
        _ada_kernel,
        grid=(n // tn,),
        in_specs=[pl.BlockSpec((m, d), lambda j: (0, 0)),
                  pl.BlockSpec((d, tn), lambda j: (0, j)),
                  pl.BlockSpec((1, tn), lambda j: (0, j))],
        out_specs=pl.BlockSpec((m, tn), lambda j: (0, j)),
        out_shape=jax.ShapeDtypeStruct((m, n), F32),
        compiler_params=_params("arbitrary"),
        name="ada",
    )(c_all, w, b.reshape(1, n))


def _norm_mod(x, nw, shift, scale):
    y = x * lax.rsqrt(jnp.mean(x * x, axis=-1, keepdims=True) + EPS)
    y = y * nw
    return y * (1.0 + scale) + shift


HIST = SUBLANES
QKV_W = 2 * DN_DK + DN_DV
REST_PIECES = 5
REST_W = REST_PIECES * LANES


def _proj_kernel(x_ref, nw_ref, sh_ref, sc_ref, w_ref, wg_ref, cw_ref, conv0_ref,
                 pqkv_ref, prest_ref, pg_ref, convn_ref, h_ref, pe_ref, tail_ref,
                 *, seg, tiles_per_batch):
    i = pl.program_id(0)
    j = pl.program_id(1)
    tm = pqkv_ref.shape[0]
    nseg = tm // seg

    @pl.when(j == 0)
    def _():
        h = _norm_mod(x_ref[...], nw_ref[...], sh_ref[0], sc_ref[0])
        h_ref[...] = h.astype(BF16)
        pg_ref[...] = jnp.dot(h_ref[...], wg_ref[...], preferred_element_type=F32)

    acc = jnp.dot(h_ref[...], w_ref[...], preferred_element_type=F32)
    prest_ref[...] = acc[:, QKV_W:]

    for s in range(nseg):
        rows = acc[s * seg:(s + 1) * seg, 0:QKV_W]
        hist = jnp.concatenate([jnp.zeros((HIST - (CONV_W - 1), QKV_W), F32), conv0_ref[s]], axis=0)
        if tiles_per_batch > 1:
            hist = jnp.where(i % tiles_per_batch == 0, hist, tail_ref[j])
            tail_ref[j] = rows[seg - HIST:]
        pe_ref[s, 0:HIST, :] = hist
        pe_ref[s, HIST:HIST + seg, :] = rows
        convn_ref[s] = rows[seg - (CONV_W - 1):]
        out = pe_ref[s, pl.ds(HIST, seg), :] * cw_ref[CONV_W - 1:CONV_W, :]
        for t in range(1, CONV_W):
            out = out + pe_ref[s, pl.ds(HIST - t, seg), :] * cw_ref[CONV_W - 1 - t:CONV_W - t, :]
        out = _silu(out)
        q = out[:, 0:DN_DK]
        k = out[:, DN_DK:2 * DN_DK]
        q = q * lax.rsqrt(jnp.sum(q * q, axis=-1, keepdims=True) + EPS) * (DN_DK ** -0.5)
        k = k * lax.rsqrt(jnp.sum(k * k, axis=-1, keepdims=True) + EPS)
        pqkv_ref[s * seg:(s + 1) * seg, 0:DN_DK] = q
        pqkv_ref[s * seg:(s + 1) * seg, DN_DK:2 * DN_DK] = k
        pqkv_ref[s * seg:(s + 1) * seg, 2 * DN_DK:] = out[:, 2 * DN_DK:]


def _in_proj(x, nw, sh, sc, mod_map, w_main, w_gate, conv_w, conv0, t, tm):
    m, d = x.shape
    tn = QKV_W + REST_W
    nblk = w_main.shape[1] // tn
    r = sh.shape[1]
    assert conv_w.shape[1] == nblk * QKV_W
    seg = min(tm, t)
    assert tm % seg == 0 and t % seg == 0 and seg >= HIST
    nseg = tm // seg
    tiles_per_batch = t // seg
    assert nseg == 1 or tiles_per_batch == 1
    batch_blk = (lambda i: i // tiles_per_batch) if nseg == 1 else (lambda i: i)
    kern = functools.partial(_proj_kernel, seg=seg, tiles_per_batch=tiles_per_batch)
    p_qkv, p_rest, p_gate, conv_seg = pl.pallas_call(
        kern,
        grid=(m // tm, nblk),
        in_specs=[pl.BlockSpec((tm, d), lambda i, j: (i, 0)),
                  pl.BlockSpec((1, d), lambda i, j: (0, 0)),
                  pl.BlockSpec((1, r, d), lambda i, j: (mod_map(i, 0), 0, 0)),
                  pl.BlockSpec((1, r, d), lambda i, j: (mod_map(i, 1), 0, 0)),
                  pl.BlockSpec((d, tn), lambda i, j: (0, j)),
                  pl.BlockSpec((d, LANES), lambda i, j: (0, 0)),
                  pl.BlockSpec((CONV_W, QKV_W), lambda i, j: (0, j)),
                  pl.BlockSpec((nseg, CONV_W - 1, QKV_W), lambda i, j: (batch_blk(i), 0, j))],
        out_specs=[pl.BlockSpec((tm, QKV_W), lambda i, j: (i, j)),
                   pl.BlockSpec((tm, REST_W), lambda i, j: (i, j)),
                   pl.BlockSpec((tm, LANES), lambda i, j: (i, 0)),
                   pl.BlockSpec((nseg, CONV_W - 1, QKV_W), lambda i, j: (i, 0, j))],
        out_shape=[jax.ShapeDtypeStruct((m, nblk * QKV_W), F32),
                   jax.ShapeDtypeStruct((m, nblk * REST_W), F32),
                   jax.ShapeDtypeStruct((m, LANES), F32),
                   jax.ShapeDtypeStruct((m // seg,) + conv0.shape[1:], F32)],
        scratch_shapes=[pltpu.VMEM((tm, d), BF16),
                        pltpu.VMEM((nseg, HIST + seg, QKV_W), F32),
                        pltpu.VMEM((nblk, HIST, QKV_W), F32)],
        compiler_params=_params("arbitrary", "arbitrary"),
        name="in_proj",
    )(x, nw, sh, sc, w_main, w_gate, conv_w, conv0)
    conv_n = conv_seg.reshape((m // t, tiles_per_batch) + conv0.shape[1:])[:, -1]
    return p_qkv, p_rest, p_gate, conv_n


def _split(a):
    hi = a.astype(BF16)
    lo = (a - hi.astype(F32)).astype(BF16)
    return hi, lo


def _dot_split(a_parts, b_parts):
    ah, al = a_parts
    bh, bl = b_parts
    rows = ah.shape[0]
    top = jnp.dot(jnp.concatenate([ah, al], axis=0), bh, preferred_element_type=F32)
    return top[:rows] + top[rows:] + jnp.dot(ah, bl, preferred_element_type=F32)


def _neumann_correction(ms, fillers=()):
    c = ms[0].shape[0]
    fillers = list(fillers)

    def fill():
        if fillers:
            fillers.pop(0)()

    ns = [-m for m in ms]
    pws = [_dot_split(_split(m), _split(m)) for m in ms]
    fill()
    k = 2
    while k < c:
        pps = [_split(pw) for pw in pws]
        nps = [_split(n) for n in ns]
        if 2 * k < c:
            tops = [_dot_split((jnp.concatenate([pp[0], npart[0]], axis=0),
                                jnp.concatenate([pp[1], npart[1]], axis=0)), pp)
                    for pp, npart in zip(pps, nps)]
            fill()
            ns = [n + pw + top[c:] for n, pw, top in zip(ns, pws, tops)]
            pws = [top[:c] for top in tops]
        else:
            prods = [_dot_split(npart, pp) for npart, pp in zip(nps, pps)]
            fill()
            ns = [n + pw + prod for n, pw, prod in zip(ns, pws, prods)]
        k *= 2
    while fillers:
        fill()
    return ns


def _mixer_kernel(lg_ref, pqkv_ref, pg_ref, prest_ref, gp_ref, dnw_ref, inv_ref, dn0_ref, ret0_ref,
                  mix_ref, dn_ref, ret_ref,
                  m_s, qkg_s, rhs_s, qd_s, kd_s, cd_s,
                  *, chunk, nb, dn_heads, ret_heads, pos0):
    c = chunk
    n = pl.program_id(1)
    batches = range(nb)
    units = [(b, h) for b in batches for h in range(dn_heads)]
    runits = [(b, h) for b in batches for h in range(ret_heads)]
    uid = lambda u: u[0] * dn_heads + u[1]
    wr = n % 2
    rd = 1 - wr

    @pl.when(n == 0)
    def _():
        m_s[...] = jnp.zeros_like(m_s)
        qkg_s[...] = jnp.zeros_like(qkg_s)
        rhs_s[...] = jnp.zeros_like(rhs_s)
        qd_s[...] = jnp.zeros_like(qd_s)
        kd_s[...] = jnp.zeros_like(kd_s)
        cd_s[...] = jnp.zeros_like(cd_s)
        dn_ref[...] = dn0_ref[...]
        ret_ref[...] = ret0_ref[...]

    row = lax.broadcasted_iota(jnp.int32, (c, c), 0)
    col = lax.broadcasted_iota(jnp.int32, (c, c), 1)
    causal = row >= col
    strict = row > col

    mix_off = dn_heads * DN_DV

    beta_all, cs = [], []
    for b in batches:
        pg = pg_ref[b]
        beta_all.append(jax.nn.sigmoid(pg))
        g_all = gp_ref[0:1, :] * jax.nn.softplus(pg + gp_ref[1:2, :])
        g_hi, g_lo = _split(g_all)
        g_lo2 = (g_all - g_hi.astype(F32) - g_lo.astype(F32)).astype(BF16)
        cs.append(jnp.dot(causal.astype(BF16), jnp.concatenate([g_hi, g_lo, g_lo2], axis=-1),
                          preferred_element_type=F32))

    prep = {}

    def prep_values():
        for b in batches:
            decay_all = cs[b][:, :LANES] + cs[b][:, LANES:2 * LANES] + cs[b][:, 2 * LANES:]
            decay_t = decay_all.T
            for h in range(dn_heads):
                i = uid((b, h))
                q = pqkv_ref[b, :, h * QKV_W:h * QKV_W + DN_DK]
                k = pqkv_ref[b, :, h * QKV_W + DN_DK:h * QKV_W + 2 * DN_DK]
                v = pqkv_ref[b, :, h * QKV_W + 2 * DN_DK:(h + 1) * QKV_W]
                beta = beta_all[b][:, h:h + 1]
                dcol = decay_all[:, dn_heads + h:dn_heads + h + 1]
                drow = decay_t[dn_heads + h:dn_heads + h + 1, :]
                last = drow[:, c - 1:c]
                kb = k * beta
                edec = jnp.exp(dcol)
                prep[i] = dict(k=k, lhs=jnp.concatenate([kb, q], axis=0).astype(BF16),
                               gam=jnp.where(causal, jnp.exp(jnp.where(causal, dcol - drow, 0.0)), 0.0))
                rhs_s[wr, i] = jnp.concatenate([v * beta, kb * edec], axis=-1)
                qd_s[wr, i] = (q * edec).astype(BF16)
                kd_s[wr, i] = (k * jnp.exp(last - dcol)).astype(BF16)
                cd_s[wr, i] = jnp.broadcast_to(jnp.exp(last), (1, LANES))

    def prep_matmuls():
        for u in units:
            p = prep[uid(u)]
            p["both"] = _dot_nt(p["lhs"], p["k"])

    def prep_store():
        for u in units:
            i = uid(u)
            both, gam = prep[i]["both"], prep[i]["gam"]
            m_s[wr, i] = jnp.where(strict, both[:c] * gam, 0.0)
            qkg_s[wr, i] = (both[c:] * gam).astype(BF16)

    tok = lax.broadcasted_iota(jnp.int32, (c, 1), 0)
    idx = tok.astype(F32)
    ret = {}

    def piece(b, blk, kind):
        off = blk * REST_W + kind * LANES
        return prest_ref[b, :, off:off + LANES]

    def rot(b, h, kind, cos, sin):
        x1 = piece(b, 2 * h, kind)
        x2 = piece(b, 2 * h + 1, kind)
        return jnp.concatenate([x1 * cos - x2 * sin, x2 * cos + x1 * sin], axis=-1)

    def wide(b, h, kind):
        return jnp.concatenate([piece(b, 2 * h, kind), piece(b, 2 * h + 1, kind)], axis=-1)

    def ret_values():
        pos = (pos0 + (n - 1) * c + tok).astype(F32)
        ang = pos * inv_ref[...]
        cos = jnp.cos(ang)
        sin = jnp.sin(ang)
        for u in runits:
            b, h = u
            lg = lg_ref[h]
            rq = rot(b, h, 1, cos, sin)
            rk = rot(b, h, 2, cos, sin) * (RET_DK ** -0.5)
            ret[u] = dict(rq=rq.astype(BF16), rk=rk.astype(BF16),
                          rv=wide(b, h, 3).astype(BF16),
                          qx=(rq * jnp.exp((idx + 1.0) * lg)).astype(BF16),
                          kz=(rk * jnp.exp((c - 1.0 - idx) * lg)).astype(BF16))

    def ret_scores():
        for u in runits:
            ret[u]["qk"] = _dot_nt(ret[u]["rq"], ret[u]["rk"])

    def ret_inner():
        dpos = (row - col).astype(F32)
        dmask = [jnp.where(causal, jnp.exp(jnp.where(causal, dpos, 0.0) * lg_ref[h]), 0.0)
                 for h in range(ret_heads)]
        for u in runits:
            ret[u]["inner"] = jnp.dot((ret[u]["qk"] * dmask[u[1]]).astype(BF16), ret[u]["rv"],
                                      preferred_element_type=F32)

    ns = _neumann_correction([m_s[rd, uid(u)] for u in units],
                             [prep_values, prep_matmuls, prep_store, ret_values, ret_scores, ret_inner])
    rhs = [rhs_s[rd, uid(u)] for u in units]
    sols = [r + _dot_split(_split(nn), _split(r)) for r, nn in zip(rhs, ns)]

    sd = [dn_ref[b, h] for b, h in units]
    ws = [jnp.dot(jnp.concatenate([sol[:, DN_DV:].astype(BF16), qd_s[rd, uid(u)]], axis=0),
                  s.astype(BF16), preferred_element_type=F32) for u, sol, s in zip(units, sols, sd)]
    sr = [ret_ref[b, h] for b, h in runits]
    ro = [ret[u]["inner"] + jnp.dot(ret[u]["qx"], s.astype(BF16), preferred_element_type=F32)
          for u, s in zip(runits, sr)]
    rupd = [_dot_tn(ret[u]["kz"], ret[u]["rv"]) for u in runits]
    us = [(sol[:, :DN_DV] - w[:c]).astype(BF16) for sol, w in zip(sols, ws)]
    o2 = [jnp.dot(qkg_s[rd, uid(u)], uu, preferred_element_type=F32) for u, uu in zip(units, us)]
    upd = [_dot_tn(kd_s[rd, uid(u)], uu) for u, uu in zip(units, us)]
    cdr = [jnp.exp(jnp.zeros((1, 1), F32) + c * lg_ref[h]) for h in range(ret_heads)]
    for k, (b, h) in enumerate(runits):
        ret_ref[b, h] = sr[k] * cdr[h] + rupd[k]
        o = ro[k]
        o = o * lax.rsqrt(jnp.mean(o * o, axis=-1, keepdims=True) + EPS)
        gg = wide(b, h, 4)
        mix_ref[b, :, mix_off + h * RET_DV:mix_off + (h + 1) * RET_DV] = (
            (o * _silu(gg)).astype(mix_ref.dtype))
    for k, (b, h) in enumerate(units):
        dn_ref[b, h] = sd[k] * cd_s[rd, uid((b, h))][:, 0:1] + upd[k]
        o = ws[k][c:] + o2[k]
        o = o * lax.rsqrt(jnp.mean(o * o, axis=-1, keepdims=True) + EPS) * dnw_ref[...]
        zz = piece(b, h, 0)
        mix_ref[b, :, h * DN_DV:(h + 1) * DN_DV] = (o * _silu(zz)).astype(mix_ref.dtype)

    @pl.when(n == 0)
    def _():
        dn_ref[...] = dn0_ref[...]
        ret_ref[...] = ret0_ref[...]


def _mixer(p_qkv, p_rest, p_gate, gate_par, dn_norm, inv_freq, log_gamma, dn0, ret0,
           *, batch, chunk, pos0):
    m = p_qkv.shape[0]
    t = m // batch
    nchunk = t // chunk
    dn_heads = dn0.shape[1]
    ret_heads = ret0.shape[1]
    mix_w = dn_heads * DN_DV + ret_heads * RET_DV
    assert p_qkv.shape[1] == dn_heads * QKV_W and p_rest.shape[1] == dn_heads * REST_W
    assert dn_heads == 2 * ret_heads and RET_DK == 2 * LANES and RET_DV == 2 * LANES
    nb = 2 if batch % 2 == 0 else 1
    kern = functools.partial(_mixer_kernel, chunk=chunk, nb=nb, dn_heads=dn_heads,
                             ret_heads=ret_heads, pos0=pos0)
    full = lambda shape: pl.BlockSpec(shape, lambda g, n: (0,) * len(shape))
    cur = lambda n: jnp.minimum(n, nchunk - 1)
    prev = lambda n: jnp.maximum(n - 1, 0)
    state = lambda a: pl.BlockSpec((nb,) + a.shape[1:], lambda g, n: (g,) + (0,) * (a.ndim - 1))
    units = nb * dn_heads
    mixed, dn_n, ret_n = pl.pallas_call(
        kern,
        grid=(batch // nb, nchunk + 1),
        in_specs=[pl.BlockSpec(memory_space=pltpu.SMEM),
                  pl.BlockSpec((nb, chunk, p_qkv.shape[1]), lambda g, n: (g, cur(n), 0)),
                  pl.BlockSpec((nb, chunk, LANES), lambda g, n: (g, cur(n), 0)),
                  pl.BlockSpec((nb, chunk, p_rest.shape[1]), lambda g, n: (g, prev(n), 0)),
                  full(gate_par.shape), full(dn_norm.shape), full(inv_freq.shape),
                  state(dn0), state(ret0)],
        out_specs=[pl.BlockSpec((nb, chunk, mix_w), lambda g, n: (g, prev(n), 0)),
                   state(dn0), state(ret0)],
        out_shape=[jax.ShapeDtypeStruct((batch, t, mix_w), BF16),
                   jax.ShapeDtypeStruct(dn0.shape, F32),
                   jax.ShapeDtypeStruct(ret0.shape, F32)],
        scratch_shapes=[pltpu.VMEM((2, units, chunk, chunk), F32),
                        pltpu.VMEM((2, units, chunk, chunk), BF16),
                        pltpu.VMEM((2, units, chunk, DN_DV + DN_DK), F32),
                        pltpu.VMEM((2, units, chunk, DN_DK), BF16),
                        pltpu.VMEM((2, units, chunk, DN_DK), BF16),
                        pltpu.VMEM((2, units, 1, LANES), F32)],
        compiler_params=_params("parallel", "arbitrary"),
        name="mixer",
    )(log_gamma, p_qkv.reshape(batch, t, -1), p_gate.reshape(batch, t, LANES),
      p_rest.reshape(batch, t, -1), gate_par, dn_norm, inv_freq, dn0, ret0)
    return mixed.reshape(m, mix_w), dn_n, ret_n


def _ffn_kernel(x_ref, mix_ref, gm_ref, wo_ref, nw_ref, sh_ref, sc_ref, gf_ref, wg_ref, wu_ref, wd_ref,
                nf_ref, shf_ref, scf_ref, y_ref, x1_ref, h_ref):
    j = pl.program_id(1)

    @pl.when(j == 0)
    def _():
        x1 = x_ref[...] + gm_ref[0] * jnp.dot(mix_ref[...], wo_ref[...], preferred_element_type=F32)
        x1_ref[...] = x1
        h = _norm_mod(x1, nw_ref[...], sh_ref[0], sc_ref[0])
        h_ref[...] = h.astype(BF16)
        y_ref[...] = jnp.zeros_like(y_ref)

    hb = h_ref[...]
    gate = jnp.dot(hb, wg_ref[...], preferred_element_type=F32)
    up = jnp.dot(hb, wu_ref[...], preferred_element_type=F32)
    act = (_silu(gate) * up).astype(BF16)
    y_ref[...] += jnp.dot(act, wd_ref[...], preferred_element_type=F32)

    @pl.when(j == pl.num_programs(1) - 1)
    def _():
        x2 = x1_ref[...] + gf_ref[0] * y_ref[...]
        y_ref[...] = _norm_mod(x2, nf_ref[...], shf_ref[0], scf_ref[0])


def _ffn(x, mixed, w_out, nw, ada, mod_map, w_gu, w_down, nf, ada_fin, fin_map, tm, tf):
    m, d = x.shape
    kmix = mixed.shape[1]
    dff = w_down.shape[0]
    nf_blocks = dff // tf
    r = ada.shape[1]
    mod = lambda k: pl.BlockSpec((1, r, d), lambda i, j: (mod_map(i, k), 0, 0))
    fin = lambda k: pl.BlockSpec((1, r, d), lambda i, j: (fin_map(i, k), 0, 0))
    return pl.pallas_call(
        _ffn_kernel,
        grid=(m // tm, nf_blocks),
        in_specs=[pl.BlockSpec((tm, d), lambda i, j: (i, 0)),
                  pl.BlockSpec((tm, kmix), lambda i, j: (i, 0)),
                  mod(2),
                  pl.BlockSpec((kmix, d), lambda i, j: (0, 0), pipeline_mode=pl.Buffered(1)),
                  pl.BlockSpec((1, d), lambda i, j: (0, 0)),
                  mod(3), mod(4), mod(5),
                  pl.BlockSpec((d, tf), lambda i, j: (0, j)),
                  pl.BlockSpec((d, tf), lambda i, j: (0, j + nf_blocks)),
                  pl.BlockSpec((tf, d), lambda i, j: (j, 0)),
                  pl.BlockSpec((1, d), lambda i, j: (0, 0)),
                  fin(0), fin(1)],
        out_specs=pl.BlockSpec((tm, d), lambda i, j: (i, 0)),
        out_shape=jax.ShapeDtypeStruct((m, d), F32),
        scratch_shapes=[pltpu.VMEM((tm, d), F32), pltpu.VMEM((tm, d), BF16)],
        compiler_params=_params("parallel", "arbitrary"),
        name="ffn",
    )(x, mixed, ada, w_out, nw, ada, ada, ada, w_gu, w_gu, w_down, nf, ada_fin, ada_fin)


def _deinterleave(a, groups, axis):
    shp = a.shape
    per = shp[axis] // groups
    a = a.reshape(shp[:axis] + (groups, per // 2, 2) + shp[axis + 1:])
    return jnp.swapaxes(a, axis + 1, axis + 2).reshape(shp)


def _interleave(a, axis):
    shp = a.shape
    a = a.reshape(shp[:axis] + (2, shp[axis] // 2) + shp[axis + 1:])
    return jnp.swapaxes(a, axis, axis + 1).reshape(shp)


def _to_blocks(a, heads):
    shp = a.shape
    kinds = shp[-1] // (heads * LANES)
    a = a.reshape(shp[:-1] + (kinds, heads, LANES))
    return jnp.swapaxes(a, -3, -2).reshape(shp)


def _from_blocks(a, heads):
    shp = a.shape
    kinds = shp[-1] // (heads * LANES)
    a = a.reshape(shp[:-1] + (heads, kinds, LANES))
    return jnp.swapaxes(a, -3, -2).reshape(shp)


class _Tiles(NamedTuple):
    proj_rows: int
    ffn_rows: int
    ffn_cols: int


def _stream(x, ada, ada_fin, conv0, dn0, ret0, weights, *, chunk, pos0, tiles, per_row):
    (nw_mix, nw_ffn, w_main, w_gate, conv_w, gate_par, dn_norm, inv_freq, log_gamma,
     w_out, w_gu, w_down, nf) = weights
    b, t, d = x.shape
    m = b * t
    xf = x.reshape(m, d)
    if per_row:
        ada_r = jnp.repeat(ada.reshape(b, N_ADA, d), t, axis=0).transpose(1, 0, 2)
        fin_r = jnp.repeat(ada_fin.reshape(b, 2, d), t, axis=0).transpose(1, 0, 2)
        mod_map = lambda tm: (lambda i, k: k)
        fin_map = lambda tm: (lambda i, k: k)
    else:
        assert t % tiles.proj_rows == 0 and t % tiles.ffn_rows == 0
        ada_r = ada.reshape(b * N_ADA, 1, d)
        fin_r = ada_fin.reshape(b * 2, 1, d)
        mod_map = lambda tm: (lambda i, k: (i // (t // tm)) * N_ADA + k)
        fin_map = lambda tm: (lambda i, k: (i // (t // tm)) * 2 + k)

    p_qkv, p_rest, p_gate, conv_n = _in_proj(xf, nw_mix, ada_r, ada_r, mod_map(tiles.proj_rows),
                                             w_main, w_gate, conv_w, conv0, t, tiles.proj_rows)
    mixed, dn_n, ret_n = _mixer(p_qkv, p_rest, p_gate, gate_par, dn_norm, inv_freq, log_gamma,
                                dn0, ret0, batch=b, chunk=chunk, pos0=pos0)
    y = _ffn(xf, mixed, w_out, nw_ffn, ada_r, mod_map(tiles.ffn_rows), w_gu, w_down, nf, fin_r,
             fin_map(tiles.ffn_rows), tiles.ffn_rows, tiles.ffn_cols)
    return y.reshape(b, t, d), conv_n, dn_n, ret_n


def kernel(x_prompt, x_sample, state_conv, state_delta, state_ret, c_prompt, c_sample, norm_mix, norm_ffn, w_ada, b_ada, w_in, conv_w, dn_a_log, dn_dt_bias, dn_norm, w_out, w_gu, w_down, norm_final, w_ada_final, b_ada_final):
    bp, tp, d = x_prompt.shape
    bs, ts, _ = x_sample.shape
    depth = w_in.shape[0]
    assert depth == 1, "single-layer pipeline"
    dn_heads = state_delta.shape[2]
    ret_heads = state_ret.shape[2]
    conv_ch = conv_w.shape[2]
    assert 2 * dn_heads <= LANES

    nb = bp + bs
    pad = (-nb) % (2 * SUBLANES)
    c_all = jnp.concatenate([c_prompt, c_sample, jnp.zeros((pad, d), F32)], axis=0)
    ada_all = _ada(c_all, w_ada[0], b_ada[0], 1024)
    fin_all = _ada(c_all, w_ada_final, b_ada_final, 1024)

    l = 0
    g_off = conv_ch + dn_heads * DN_DV
    r_off = g_off + 2 * dn_heads
    rw = ret_heads * RET_DK
    w_l = w_in[l]
    w_kinds = jnp.concatenate([w_l[:, :g_off],
                               _deinterleave(w_l[:, r_off:r_off + rw], ret_heads, axis=1),
                               _deinterleave(w_l[:, r_off + rw:r_off + 2 * rw], ret_heads, axis=1),
                               w_l[:, r_off + 2 * rw:]], axis=1)
    w_main = _to_blocks(w_kinds, dn_heads).astype(BF16)
    w_gate = jnp.concatenate([w_l[:, g_off:r_off],
                              jnp.zeros((d, LANES - 2 * dn_heads), F32)], axis=1).astype(BF16)
    lane_pad = jnp.zeros((LANES - 2 * dn_heads,), F32)
    gate_par = jnp.stack([
        jnp.concatenate([jnp.zeros((dn_heads,), F32), -jnp.exp(dn_a_log[l].astype(F32)), lane_pad]),
        jnp.concatenate([jnp.zeros((dn_heads,), F32), dn_dt_bias[l].astype(F32), lane_pad])])
    inv_freq = (1.0 / (ROPE_BASE ** jnp.linspace(0.0, 1.0, RET_DK // 2, dtype=F32))).reshape(1, -1)
    log_gamma = jnp.log(1.0 - 2.0 ** (-5.0 - jnp.arange(ret_heads, dtype=F32)))
    weights = (norm_mix[l].reshape(1, d), norm_ffn[l].reshape(1, d), w_main, w_gate,
               _to_blocks(conv_w[l], dn_heads),
               gate_par, dn_norm[l].reshape(1, -1), inv_freq, log_gamma,
               w_out[l].astype(BF16), w_gu[l].astype(BF16), w_down[l].astype(BF16),
               norm_final.reshape(1, d))

    zc = jnp.zeros((bp, CONV_W - 1, conv_ch), F32)
    zd = jnp.zeros((bp,) + state_delta.shape[2:], F32)
    zr = jnp.zeros((bp,) + state_ret.shape[2:], F32)

    y_p, cp, dp, rp = _stream(x_prompt, ada_all[:bp], fin_all[:bp], zc, zd, zr, weights,
                              chunk=CHUNK, pos0=0, per_row=False,
                              tiles=_Tiles(proj_rows=1024, ffn_rows=512, ffn_cols=512))
    rs_in = _deinterleave(state_ret[l], 1, axis=2)
    y_s, cs, ds, rs = _stream(x_sample, ada_all[bp:nb], fin_all[bp:nb],
                              _to_blocks(state_conv[l], dn_heads), state_delta[l], rs_in, weights,
                              chunk=ts, pos0=PAST_LEN, per_row=True,
                              tiles=_Tiles(proj_rows=bs * ts, ffn_rows=bs * ts, ffn_cols=512))
    cp, cs = _from_blocks(cp, dn_heads), _from_blocks(cs, dn_heads)
    rp, rs = _interleave(rp, axis=2), _interleave(rs, axis=2)
    return (y_p, y_s, cp[None], dp[None], rp[None], cs[None], ds[None], rs[None])
```

```python
import functools
from typing import NamedTuple

import jax
import jax.numpy as jnp
from jax import lax
from jax.experimental import pallas as pl
from jax.experimental.pallas import tpu as pltpu

F32 = jnp.float32
BF16 = jnp.bfloat16

DN_DK = 128
DN_DV = 128
RET_DK = 256
RET_DV = 256
CONV_W = 4
N_ADA = 6
CHUNK = 64
PAST_LEN = 2048
ROPE_BASE = 10000.0
EPS = 1e-6

LANES = 128
SUBLANES = 8
VMEM_LIMIT = 56 * 1024 * 1024


def _silu(x):
    return x * jax.nn.sigmoid(x)


def _dot(a, b):
    return jnp.dot(a.astype(BF16), b.astype(BF16), preferred_element_type=F32)


def _dot_nt(a, b):
    return lax.dot_general(a.astype(BF16), b.astype(BF16), (((1,), (1,)), ((), ())),
                           preferred_element_type=F32)


def _dot_tn(a, b):
    return lax.dot_general(a.astype(BF16), b.astype(BF16), (((0,), (0,)), ((), ())),
                           preferred_element_type=F32)


def _params(*sem):
    return pltpu.CompilerParams(dimension_semantics=sem, vmem_limit_bytes=VMEM_LIMIT)


def _ada_kernel(c_ref, w_ref, b_ref, o_ref):
    s = _silu(c_ref[...])
    o_ref[...] = _dot(s, w_ref[...]) + b_ref[...]


def _ada(c_all, w, b, tn):
    m, d = c_all.shape
    n = w.shape[1]
    return pl.pallas_call(
        _ada_kernel,
        grid=(n // tn,),
        in_specs=[pl.BlockSpec((m, d), lambda j: (0, 0)),
                  pl.BlockSpec((d, tn), lambda j: (0, j)),
                  pl.BlockSpec((1, tn), lambda j: (0, j))],
        out_specs=pl.BlockSpec((m, tn), lambda j: (0, j)),
        out_shape=jax.ShapeDtypeStruct((m, n), F32),
        compiler_params=_params("arbitrary"),
        name="ada",
    )(c_all, w, b.reshape(1, n))


def _norm_mod(x, nw, shift, scale):
    y = x * lax.rsqrt(jnp.mean(x * x, axis=-1, keepdims=True) + EPS)
    y = y * nw
    return y * (1.0 + scale) + shift


HIST = SUBLANES
QKV_W = 2 * DN_DK + DN_DV
REST_PIECES = 5
REST_W = REST_PIECES * LANES


def _proj_kernel(x_ref, nw_ref, sh_ref, sc_ref, w_ref, wg_ref, cw_ref, conv0_ref,
                 pqkv_ref, prest_ref, pg_ref, convn_ref, h_ref, pe_ref, tail_ref,
                 *, seg, tiles_per_batch):
    i = pl.program_id(0)
    j = pl.program_id(1)
    tm = pqkv_ref.shape[0]
    nseg = tm // seg

    @pl.when(j == 0)
    def _():
        h = _norm_mod(x_ref[...], nw_ref[...], sh_ref[0], sc_ref[0])
        h_ref[...] = h.astype(BF16)
        pg_ref[...] = jnp.dot(h_ref[...], wg_ref[...], preferred_element_type=F32)

    acc = jnp.dot(h_ref[...], w_ref[...], preferred_element_type=F32)
    prest_ref[...] = acc[:, QKV_W:]

    for s in range(nseg):
        rows = acc[s * seg:(s + 1) * seg, 0:QKV_W]
        hist = jnp.concatenate([jnp.zeros((HIST - (CONV_W - 1), QKV_W), F32), conv0_ref[s]], axis=0)
        if tiles_per_batch > 1:
            hist = jnp.where(i % tiles_per_batch == 0, hist, tail_ref[j])
            tail_ref[j] = rows[seg - HIST:]
        pe_ref[s, 0:HIST, :] = hist
        pe_ref[s, HIST:HIST + seg, :] = rows
        convn_ref[s] = rows[seg - (CONV_W - 1):]
        out = pe_ref[s, pl.ds(HIST, seg), :] * cw_ref[CONV_W - 1:CONV_W, :]
        for t in range(1, CONV_W):
            out = out + pe_ref[s, pl.ds(HIST - t, seg), :] * cw_ref[CONV_W - 1 - t:CONV_W - t, :]
        out = _silu(out)
        q = out[:, 0:DN_DK]
        k = out[:, DN_DK:2 * DN_DK]
        q = q * lax.rsqrt(jnp.sum(q * q, axis=-1, keepdims=True) + EPS) * (DN_DK ** -0.5)
        k = k * lax.rsqrt(jnp.sum(k * k, axis=-1, keepdims=True) + EPS)
        pqkv_ref[s * seg:(s + 1) * seg, 0:DN_DK] = q
        pqkv_ref[s * seg:(s + 1) * seg, DN_DK:2 * DN_DK] = k
        pqkv_ref[s * seg:(s + 1) * seg, 2 * DN_DK:] = out[:, 2 * DN_DK:]


def _in_proj(x, nw, sh, sc, mod_map, w_main, w_gate, conv_w, conv0, t, tm):
    m, d = x.shape
    tn = QKV_W + REST_W
    nblk = w_main.shape[1] // tn
    r = sh.shape[1]
    assert conv_w.shape[1] == nblk * QKV_W
    seg = min(tm, t)
    assert tm % seg == 0 and t % seg == 0 and seg >= HIST
    nseg = tm // seg
    tiles_per_batch = t // seg
    assert nseg == 1 or tiles_per_batch == 1
    batch_blk = (lambda i: i // tiles_per_batch) if nseg == 1 else (lambda i: i)
    kern = functools.partial(_proj_kernel, seg=seg, tiles_per_batch=tiles_per_batch)
    p_qkv, p_rest, p_gate, conv_seg = pl.pallas_call(
        kern,
        grid=(m // tm, nblk),
        in_specs=[pl.BlockSpec((tm, d), lambda i, j: (i, 0)),
                  pl.BlockSpec((1, d), lambda i, j: (0, 0)),
                  pl.BlockSpec((1, r, d), lambda i, j: (mod_map(i, 0), 0, 0)),
                  pl.BlockSpec((1, r, d), lambda i, j: (mod_map(i, 1), 0, 0)),
                  pl.BlockSpec((d, tn), lambda i, j: (0, j)),
                  pl.BlockSpec((d, LANES), lambda i, j: (0, 0)),
                  pl.BlockSpec((CONV_W, QKV_W), lambda i, j: (0, j)),
                  pl.BlockSpec((nseg, CONV_W - 1, QKV_W), lambda i, j: (batch_blk(i), 0, j))],
        out_specs=[pl.BlockSpec((tm, QKV_W), lambda i, j: (i, j)),
                   pl.BlockSpec((tm, REST_W), lambda i, j: (i, j)),
                   pl.BlockSpec((tm, LANES), lambda i, j: (i, 0)),
                   pl.BlockSpec((nseg, CONV_W - 1, QKV_W), lambda i, j: (i, 0, j))],
        out_shape=[jax.ShapeDtypeStruct((m, nblk * QKV_W), F32),
                   jax.ShapeDtypeStruct((m, nblk * REST_W), F32),
                   jax.ShapeDtypeStruct((m, LANES), F32),
                   jax.ShapeDtypeStruct((m // seg,) + conv0.shape[1:], F32)],
        scratch_shapes=[pltpu.VMEM((tm, d), BF16),
                        pltpu.VMEM((nseg, HIST + seg, QKV_W), F32),
                        pltpu.VMEM((nblk, HIST, QKV_W), F32)],
        compiler_params=_params("arbitrary", "arbitrary"),
        name="in_proj",
    )(x, nw, sh, sc, w_main, w_gate, conv_w, conv0)
    conv_n = conv_seg.reshape((m // t, tiles_per_batch) + conv0.shape[1:])[:, -1]
    return p_qkv, p_rest, p_gate, conv_n


def _split(a):
    hi = a.astype(BF16)
    lo = (a - hi.astype(F32)).astype(BF16)
    return hi, lo


def _dot_split(a_parts, b_parts):
    ah, al = a_parts
    bh, bl = b_parts
    rows = ah.shape[0]
    top = jnp.dot(jnp.concatenate([ah, al], axis=0), bh, preferred_element_type=F32)
    return top[:rows] + top[rows:] + jnp.dot(ah, bl, preferred_element_type=F32)


def _neumann_correction(ms, fillers=()):
    c = ms[0].shape[0]
    fillers = list(fillers)

    def fill():
        if fillers:
            fillers.pop(0)()

    ns = [-m for m in ms]
    pws = [_dot_split(_split(m), _split(m)) for m in ms]
    fill()
    k = 2
    while k < c:
        pps = [_split(pw) for pw in pws]
        nps = [_split(n) for n in ns]
        if 2 * k < c:
            tops = [_dot_split((jnp.concatenate([pp[0], npart[0]], axis=0),
                                jnp.concatenate([pp[1], npart[1]], axis=0)), pp)
                    for pp, npart in zip(pps, nps)]
            fill()
            ns = [n + pw + top[c:] for n, pw, top in zip(ns, pws, tops)]
            pws = [top[:c] for top in tops]
        else:
            prods = [_dot_split(npart, pp) for npart, pp in zip(nps, pps)]
            fill()
            ns = [n + pw + prod for n, pw, prod in zip(ns, pws, prods)]
        k *= 2
    while fillers:
        fill()
    return ns


def _mixer_kernel(lg_ref, pqkv_ref, pg_ref, prest_ref, gp_ref, dnw_ref, inv_ref, dn0_ref, ret0_ref,
                  mix_ref, dn_ref, ret_ref,
                  m_s, qkg_s, rhs_s, qd_s, kd_s, cd_s,
                  *, chunk, nb, dn_heads, ret_heads, pos0):
    c = chunk
    n = pl.program_id(1)
    batches = range(nb)
    units = [(b, h) for b in batches for h in range(dn_heads)]
    runits = [(b, h) for b in batches for h in range(ret_heads)]
    uid = lambda u: u[0] * dn_heads + u[1]
    wr = n % 2
    rd = 1 - wr

    @pl.when(n == 0)
    def _():
        m_s[...] = jnp.zeros_like(m_s)
        qkg_s[...] = jnp.zeros_like(qkg_s)
        rhs_s[...] = jnp.zeros_like(rhs_s)
        qd_s[...] = jnp.zeros_like(qd_s)
        kd_s[...] = jnp.zeros_like(kd_s)
        cd_s[...] = jnp.zeros_like(cd_s)
        dn_ref[...] = dn0_ref[...]
        ret_ref[...] = ret0_ref[...]

    row = lax.broadcasted_iota(jnp.int32, (c, c), 0)
    col = lax.broadcasted_iota(jnp.int32, (c, c), 1)
    causal = row >= col
    strict = row > col

    mix_off = dn_heads * DN_DV

    beta_all, cs = [], []
    for b in batches:
        pg = pg_ref[b]
        beta_all.append(jax.nn.sigmoid(pg))
        g_all = gp_ref[0:1, :] * jax.nn.softplus(pg + gp_ref[1:2, :])
        g_hi, g_lo = _split(g_all)
        g_lo2 = (g_all - g_hi.astype(F32) - g_lo.astype(F32)).astype(BF16)
        cs.append(jnp.dot(causal.astype(BF16), jnp.concatenate([g_hi, g_lo, g_lo2], axis=-1),
                          preferred_element_type=F32))

    prep = {}

    def prep_values():
        for b in batches:
            decay_all = cs[b][:, :LANES] + cs[b][:, LANES:2 * LANES] + cs[b][:, 2 * LANES:]
            decay_t = decay_all.T
            for h in range(dn_heads):
                i = uid((b, h))
                q = pqkv_ref[b, :, h * QKV_W:h * QKV_W + DN_DK]
                k = pqkv_ref[b, :, h * QKV_W + DN_DK:h * QKV_W + 2 * DN_DK]
                v = pqkv_ref[b, :, h * QKV_W + 2 * DN_DK:(h + 1) * QKV_W]
                beta = beta_all[b][:, h:h + 1]
                dcol = decay_all[:, dn_heads + h:dn_heads + h + 1]
                drow = decay_t[dn_heads + h:dn_heads + h + 1, :]
                last = drow[:, c - 1:c]
                kb = k * beta
                edec = jnp.exp(dcol)
                prep[i] = dict(k=k, lhs=jnp.concatenate([kb, q], axis=0).astype(BF16),
                               gam=jnp.where(causal, jnp.exp(jnp.where(causal, dcol - drow, 0.0)), 0.0))
                rhs_s[wr, i] = jnp.concatenate([v * beta, kb * edec], axis=-1)
                qd_s[wr, i] = (q * edec).astype(BF16)
                kd_s[wr, i] = (k * jnp.exp(last - dcol)).astype(BF16)
                cd_s[wr, i] = jnp.broadcast_to(jnp.exp(last), (1, LANES))

    def prep_matmuls():
        for u in units:
            p = prep[uid(u)]
            p["both"] = _dot_nt(p["lhs"], p["k"])

    def prep_store():
        for u in units:
            i = uid(u)
            both, gam = prep[i]["both"], prep[i]["gam"]
            m_s[wr, i] = jnp.where(strict, both[:c] * gam, 0.0)
            qkg_s[wr, i] = (both[c:] * gam).astype(BF16)

    tok = lax.broadcasted_iota(jnp.int32, (c, 1), 0)
    idx = tok.astype(F32)
    ret = {}

    def piece(b, blk, kind):
        off = blk * REST_W + kind * LANES
        return prest_ref[b, :, off:off + LANES]

    def rot(b, h, kind, cos, sin):
        x1 = piece(b, 2 * h, kind)
        x2 = piece(b, 2 * h + 1, kind)
        return jnp.concatenate([x1 * cos - x2 * sin, x2 * cos + x1 * sin], axis=-1)

    def wide(b, h, kind):
        return jnp.concatenate([piece(b, 2 * h, kind), piece(b, 2 * h + 1, kind)], axis=-1)

    def ret_values():
        pos = (pos0 + (n - 1) * c + tok).astype(F32)
        ang = pos * inv_ref[...]
        cos = jnp.cos(ang)
        sin = jnp.sin(ang)
        for u in runits:
            b, h = u
            lg = lg_ref[h]
            rq = rot(b, h, 1, cos, sin)
            rk = rot(b, h, 2, cos, sin) * (RET_DK ** -0.5)
            ret[u] = dict(rq=rq.astype(BF16), rk=rk.astype(BF16),
                          rv=wide(b, h, 3).astype(BF16),
                          qx=(rq * jnp.exp((idx + 1.0) * lg)).astype(BF16),
                          kz=(rk * jnp.exp((c - 1.0 - idx) * lg)).astype(BF16))

    def ret_scores():
        for u in runits:
            ret[u]["qk"] = _dot_nt(ret[u]["rq"], ret[u]["rk"])

    def ret_inner():
        dpos = (row - col).astype(F32)
        dmask = [jnp.where(causal, jnp.exp(jnp.where(causal, dpos, 0.0) * lg_ref[h]), 0.0)
                 for h in range(ret_heads)]
        for u in runits:
            ret[u]["inner"] = jnp.dot((ret[u]["qk"] * dmask[u[1]]).astype(BF16), ret[u]["rv"],
                                      preferred_element_type=F32)

    ns = _neumann_correction([m_s[rd, uid(u)] for u in units],
                             [prep_values, prep_matmuls, prep_store, ret_values, ret_scores, ret_inner])
    rhs = [rhs_s[rd, uid(u)] for u in units]
    sols = [r + _dot_split(_split(nn), _split(r)) for r, nn in zip(rhs, ns)]

    sd = [dn_ref[b, h] for b, h in units]
    ws = [jnp.dot(jnp.concatenate([sol[:, DN_DV:].astype(BF16), qd_s[rd, uid(u)]], axis=0),
                  s.astype(BF16), preferred_element_type=F32) for u, sol, s in zip(units, sols, sd)]
    sr = [ret_ref[b, h] for b, h in runits]
    ro = [ret[u]["inner"] + jnp.dot(ret[u]["qx"], s.astype(BF16), preferred_element_type=F32)
          for u, s in zip(runits, sr)]
    rupd = [_dot_tn(ret[u]["kz"], ret[u]["rv"]) for u in runits]
    us = [(sol[:, :DN_DV] - w[:c]).astype(BF16) for sol, w in zip(sols, ws)]
    o2 = [jnp.dot(qkg_s[rd, uid(u)], uu, preferred_element_type=F32) for u, uu in zip(units, us)]
    upd = [_dot_tn(kd_s[rd, uid(u)], uu) for u, uu in zip(units, us)]
    cdr = [jnp.exp(jnp.zeros((1, 1), F32) + c * lg_ref[h]) for h in range(ret_heads)]
    for k, (b, h) in enumerate(runits):
        ret_ref[b, h] = sr[k] * cdr[h] + rupd[k]
        o = ro[k]
        o = o * lax.rsqrt(jnp.mean(o * o, axis=-1, keepdims=True) + EPS)
        gg = wide(b, h, 4)
        mix_ref[b, :, mix_off + h * RET_DV:mix_off + (h + 1) * RET_DV] = (
            (o * _silu(gg)).astype(mix_ref.dtype))
    for k, (b, h) in enumerate(units):
        dn_ref[b, h] = sd[k] * cd_s[rd, uid((b, h))][:, 0:1] + upd[k]
        o = ws[k][c:] + o2[k]
        o = o * lax.rsqrt(jnp.mean(o * o, axis=-1, keepdims=True) + EPS) * dnw_ref[...]
        zz = piece(b, h, 0)
        mix_ref[b, :, h * DN_DV:(h + 1) * DN_DV] = (o * _silu(zz)).astype(mix_ref.dtype)

    @pl.when(n == 0)
    def _():
        dn_ref[...] = dn0_ref[...]
        ret_ref[...] = ret0_ref[...]


def _mixer(p_qkv, p_rest, p_gate, gate_par, dn_norm, inv_freq, log_gamma, dn0, ret0,
           *, batch, chunk, pos0):
    m = p_qkv.shape[0]
    t = m // batch
    nchunk = t // chunk
    dn_heads = dn0.shape[1]
    ret_heads = ret0.shape[1]
    mix_w = dn_heads * DN_DV + ret_heads * RET_DV
    assert p_qkv.shape[1] == dn_heads * QKV_W and p_rest.shape[1] == dn_heads * REST_W
    assert dn_heads == 2 * ret_heads and RET_DK == 2 * LANES and RET_DV == 2 * LANES
    nb = 2 if batch % 2 == 0 else 1
    kern = functools.partial(_mixer_kernel, chunk=chunk, nb=nb, dn_heads=dn_heads,
                             ret_heads=ret_heads, pos0=pos0)
    full = lambda shape: pl.BlockSpec(shape, lambda g, n: (0,) * len(shape))
    cur = lambda n: jnp.minimum(n, nchunk - 1)
    prev = lambda n: jnp.maximum(n - 1, 0)
    state = lambda a: pl.BlockSpec((nb,) + a.shape[1:], lambda g, n: (g,) + (0,) * (a.ndim - 1))
    units = nb * dn_heads
    mixed, dn_n, ret_n = pl.pallas_call(
        kern,
        grid=(batch // nb, nchunk + 1),
        in_specs=[pl.BlockSpec(memory_space=pltpu.SMEM),
                  pl.BlockSpec((nb, chunk, p_qkv.shape[1]), lambda g, n: (g, cur(n), 0)),
                  pl.BlockSpec((nb, chunk, LANES), lambda g, n: (g, cur(n), 0)),
                  pl.BlockSpec((nb, chunk, p_rest.shape[1]), lambda g, n: (g, prev(n), 0)),
                  full(gate_par.shape), full(dn_norm.shape), full(inv_freq.shape),
                  state(dn0), state(ret0)],
        out_specs=[pl.BlockSpec((nb, chunk, mix_w), lambda g, n: (g, prev(n), 0)),
                   state(dn0), state(ret0)],
        out_shape=[jax.ShapeDtypeStruct((batch, t, mix_w), BF16),
                   jax.ShapeDtypeStruct(dn0.shape, F32),
                   jax.ShapeDtypeStruct(ret0.shape, F32)],
        scratch_shapes=[pltpu.VMEM((2, units, chunk, chunk), F32),
                        pltpu.VMEM((2, units, chunk, chunk), BF16),
                        pltpu.VMEM((2, units, chunk, DN_DV + DN_DK), F32),
                        pltpu.VMEM((2, units, chunk, DN_DK), BF16),
                        pltpu.VMEM((2, units, chunk, DN_DK), BF16),
                        pltpu.VMEM((2, units, 1, LANES), F32)],
        compiler_params=_params("parallel", "arbitrary"),
        name="mixer",
    )(log_gamma, p_qkv.reshape(batch, t, -1), p_gate.reshape(batch, t, LANES),
      p_rest.reshape(batch, t, -1), gate_par, dn_norm, inv_freq, dn0, ret0)
    return mixed.reshape(m, mix_w), dn_n, ret_n


def _ffn_kernel(x_ref, mix_ref, gm_ref, wo_ref, nw_ref, sh_ref, sc_ref, gf_ref, wg_ref, wu_ref, wd_ref,
                nf_ref, shf_ref, scf_ref, y_ref, x1_ref, h_ref):
    j = pl.program_id(1)

    @pl.when(j == 0)
    def _():
        x1 = x_ref[...] + gm_ref[0] * jnp.dot(mix_ref[...], wo_ref[...], preferred_element_type=F32)
        x1_ref[...] = x1
        h = _norm_mod(x1, nw_ref[...], sh_ref[0], sc_ref[0])
        h_ref[...] = h.astype(BF16)
        y_ref[...] = jnp.zeros_like(y_ref)

    hb = h_ref[...]
    gate = jnp.dot(hb, wg_ref[...], preferred_element_type=F32)
    up = jnp.dot(hb, wu_ref[...], preferred_element_type=F32)
    act = (_silu(gate) * up).astype(BF16)
    y_ref[...] += jnp.dot(act, wd_ref[...], preferred_element_type=F32)

    @pl.when(j == pl.num_programs(1) - 1)
    def _():
        x2 = x1_ref[...] + gf_ref[0] * y_ref[...]
        y_ref[...] = _norm_mod(x2, nf_ref[...], shf_ref[0], scf_ref[0])


def _ffn(x, mixed, w_out, nw, ada, mod_map, w_gu, w_down, nf, ada_fin, fin_map, tm, tf):
    m, d = x.shape
    kmix = mixed.shape[1]
    dff = w_down.shape[0]
    nf_blocks = dff // tf
    r = ada.shape[1]
    mod = lambda k: pl.BlockSpec((1, r, d), lambda i, j: (mod_map(i, k), 0, 0))
    fin = lambda k: pl.BlockSpec((1, r, d), lambda i, j: (fin_map(i, k), 0, 0))
    return pl.pallas_call(
        _ffn_kernel,
        grid=(m // tm, nf_blocks),
        in_specs=[pl.BlockSpec((tm, d), lambda i, j: (i, 0)),
                  pl.BlockSpec((tm, kmix), lambda i, j: (i, 0)),
                  mod(2),
                  pl.BlockSpec((kmix, d), lambda i, j: (0, 0), pipeline_mode=pl.Buffered(1)),
                  pl.BlockSpec((1, d), lambda i, j: (0, 0)),
                  mod(3), mod(4), mod(5),
                  pl.BlockSpec((d, tf), lambda i, j: (0, j)),
                  pl.BlockSpec((d, tf), lambda i, j: (0, j + nf_blocks)),
                  pl.BlockSpec((tf, d), lambda i, j: (j, 0)),
                  pl.BlockSpec((1, d), lambda i, j: (0, 0)),
                  fin(0), fin(1)],
        out_specs=pl.BlockSpec((tm, d), lambda i, j: (i, 0)),
        out_shape=jax.ShapeDtypeStruct((m, d), F32),
        scratch_shapes=[pltpu.VMEM((tm, d), F32), pltpu.VMEM((tm, d), BF16)],
        compiler_params=_params("parallel", "arbitrary"),
        name="ffn",
    )(x, mixed, ada, w_out, nw, ada, ada, ada, w_gu, w_gu, w_down, nf, ada_fin, ada_fin)


def _deinterleave(a, groups, axis):
    shp = a.shape
    per = shp[axis] // groups
    a = a.reshape(shp[:axis] + (groups, per // 2, 2) + shp[axis + 1:])
    return jnp.swapaxes(a, axis + 1, axis + 2).reshape(shp)


def _interleave(a, axis):
    shp = a.shape
    a = a.reshape(shp[:axis] + (2, shp[axis] // 2) + shp[axis + 1:])
    return jnp.swapaxes(a, axis, axis + 1).reshape(shp)


def _to_blocks(a, heads):
    shp = a.shape
    kinds = shp[-1] // (heads * LANES)
    a = a.reshape(shp[:-1] + (kinds, heads, LANES))
    return jnp.swapaxes(a, -3, -2).reshape(shp)


def _from_blocks(a, heads):
    shp = a.shape
    kinds = shp[-1] // (heads * LANES)
    a = a.reshape(shp[:-1] + (heads, kinds, LANES))
    return jnp.swapaxes(a, -3, -2).reshape(shp)


class _Tiles(NamedTuple):
    proj_rows: int
    ffn_rows: int
    ffn_cols: int


def _stream(x, ada, ada_fin, conv0, dn0, ret0, weights, *, chunk, pos0, tiles, per_row):
    (nw_mix, nw_ffn, w_main, w_gate, conv_w, gate_par, dn_norm, inv_freq, log_gamma,
     w_out, w_gu, w_down, nf) = weights
    b, t, d = x.shape
    m = b * t
    xf = x.reshape(m, d)
    if per_row:
        ada_r = jnp.repeat(ada.reshape(b, N_ADA, d), t, axis=0).transpose(1, 0, 2)
        fin_r = jnp.repeat(ada_fin.reshape(b, 2, d), t, axis=0).transpose(1, 0, 2)
        mod_map = lambda tm: (lambda i, k: k)
        fin_map = lambda tm: (lambda i, k: k)
    else:
        assert t % tiles.proj_rows == 0 and t % tiles.ffn_rows == 0
        ada_r = ada.reshape(b * N_ADA, 1, d)
        fin_r = ada_fin.reshape(b * 2, 1, d)
        mod_map = lambda tm: (lambda i, k: (i // (t // tm)) * N_ADA + k)
        fin_map = lambda tm: (lambda i, k: (i // (t // tm)) * 2 + k)

    p_qkv, p_rest, p_gate, conv_n = _in_proj(xf, nw_mix, ada_r, ada_r, mod_map(tiles.proj_rows),
                                             w_main, w_gate, conv_w, conv0, t, tiles.proj_rows)
    mixed, dn_n, ret_n = _mixer(p_qkv, p_rest, p_gate, gate_par, dn_norm, inv_freq, log_gamma,
                                dn0, ret0, batch=b, chunk=chunk, pos0=pos0)
    y = _ffn(xf, mixed, w_out, nw_ffn, ada_r, mod_map(tiles.ffn_rows), w_gu, w_down, nf, fin_r,
             fin_map(tiles.ffn_rows), tiles.ffn_rows, tiles.ffn_cols)
    return y.reshape(b, t, d), conv_n, dn_n, ret_n


def kernel(x_prompt, x_sample, state_conv, state_delta, state_ret, c_prompt, c_sample, norm_mix, norm_ffn, w_ada, b_ada, w_in, conv_w, dn_a_log, dn_dt_bias, dn_norm, w_out, w_gu, w_down, norm_final, w_ada_final, b_ada_final):
    bp, tp, d = x_prompt.shape
    bs, ts, _ = x_sample.shape
    depth = w_in.shape[0]
    assert depth == 1, "single-layer pipeline"
    dn_heads = state_delta.shape[2]
    ret_heads = state_ret.shape[2]
    conv_ch = conv_w.shape[2]
    assert 2 * dn_heads <= LANES

    nb = bp + bs
    pad = (-nb) % (2 * SUBLANES)
    c_all = jnp.concatenate([c_prompt, c_sample, jnp.zeros((pad, d), F32)], axis=0)
    ada_all = _ada(c_all, w_ada[0], b_ada[0], 1024)
    fin_all = _ada(c_all, w_ada_final, b_ada_final, 1024)

    l = 0
    g_off = conv_ch + dn_heads * DN_DV
    r_off = g_off + 2 * dn_heads
    rw = ret_heads * RET_DK
    w_l = w_in[l].astype(BF16)
    w_kinds = jnp.concatenate([w_l[:, :g_off],
                               _deinterleave(w_l[:, r_off:r_off + rw], ret_heads, axis=1),
                               _deinterleave(w_l[:, r_off + rw:r_off + 2 * rw], ret_heads, axis=1),
                               w_l[:, r_off + 2 * rw:]], axis=1)
    w_main = _to_blocks(w_kinds, dn_heads)
    w_gate = jnp.concatenate([w_l[:, g_off:r_off],
                              jnp.zeros((d, LANES - 2 * dn_heads), BF16)], axis=1)
    lane_pad = jnp.zeros((LANES - 2 * dn_heads,), F32)
    gate_par = jnp.stack([
        jnp.concatenate([jnp.zeros((dn_heads,), F32), -jnp.exp(dn_a_log[l].astype(F32)), lane_pad]),
        jnp.concatenate([jnp.zeros((dn_heads,), F32), dn_dt_bias[l].astype(F32), lane_pad])])
    inv_freq = (1.0 / (ROPE_BASE ** jnp.linspace(0.0, 1.0, RET_DK // 2, dtype=F32))).reshape(1, -1)
    log_gamma = jnp.log(1.0 - 2.0 ** (-5.0 - jnp.arange(ret_heads, dtype=F32)))
    weights = (norm_mix[l].reshape(1, d), norm_ffn[l].reshape(1, d), w_main, w_gate,
               _to_blocks(conv_w[l], dn_heads),
               gate_par, dn_norm[l].reshape(1, -1), inv_freq, log_gamma,
               w_out[l].astype(BF16), w_gu[l].astype(BF16), w_down[l].astype(BF16),
               norm_final.reshape(1, d))

    zc = jnp.zeros((bp, CONV_W - 1, conv_ch), F32)
    zd = jnp.zeros((bp,) + state_delta.shape[2:], F32)
    zr = jnp.zeros((bp,) + state_ret.shape[2:], F32)

    y_p, cp, dp, rp = _stream(x_prompt, ada_all[:bp], fin_all[:bp], zc, zd, zr, weights,
                              chunk=CHUNK, pos0=0, per_row=False,
                              tiles=_Tiles(proj_rows=1024, ffn_rows=512, ffn_cols=512))
    rs_in = _deinterleave(state_ret[l], 1, axis=2)
    y_s, cs, ds, rs = _stream(x_sample, ada_all[bp:nb], fin_all[bp:nb],
                              _to_blocks(state_conv[l], dn_heads), state_delta[l], rs_in, weights,
                              chunk=ts, pos0=PAST_LEN, per_row=True,
                              tiles=_Tiles(proj_rows=bs * ts, ffn_rows=bs * ts, ffn_cols=512))
    cp, cs = _from_blocks(cp, dn_heads), _from_blocks(cs, dn_heads)
    rp, rs = _interleave(rp, axis=2), _interleave(rs, axis=2)
    return (y_p, y_s, cp[None], dp[None], rp[None], cs[None], ds[None], rs[None])
```

```python
import functools
from typing import NamedTuple

import numpy as np
import jax
import jax.numpy as jnp
from jax import lax
from jax.experimental import pallas as pl
from jax.experimental.pallas import tpu as pltpu

F32 = jnp.float32
BF16 = jnp.bfloat16

DN_DK = 128
DN_DV = 128
RET_DK = 256
RET_DV = 256
CONV_W = 4
N_ADA = 6
CHUNK = 64
PAST_LEN = 2048
ROPE_BASE = 10000.0
EPS = 1e-6

LANES = 128
SUBLANES = 8
VMEM_LIMIT = 56 * 1024 * 1024


def _silu(x):
    return x * jax.nn.sigmoid(x)


def _dot(a, b):
    return jnp.dot(a.astype(BF16), b.astype(BF16), preferred_element_type=F32)


def _dot_nt(a, b):
    return lax.dot_general(a.astype(BF16), b.astype(BF16), (((1,), (1,)), ((), ())),
                           preferred_element_type=F32)


def _dot_tn(a, b):
    return lax.dot_general(a.astype(BF16), b.astype(BF16), (((0,), (0,)), ((), ())),
                           preferred_element_type=F32)


def _params(*sem):
    return pltpu.CompilerParams(dimension_semantics=sem, vmem_limit_bytes=VMEM_LIMIT)


def _ada_kernel(c_ref, w_ref, b_ref, o_ref):
    s = _silu(c_ref[...])
    o_ref[...] = _dot(s, w_ref[...]) + b_ref[...]


def _ada(c_all, w, b, tn):
    m, d = c_all.shape
    n = w.shape[1]
    return pl.pallas_call(
        _ada_kernel,
        grid=(n // tn,),
        in_specs=[pl.BlockSpec((m, d), lambda j: (0, 0)),
                  pl.BlockSpec((d, tn), lambda j: (0, j)),
                  pl.BlockSpec((1, tn), lambda j: (0, j))],
        out_specs=pl.BlockSpec((m, tn), lambda j: (0, j)),
        out_shape=jax.ShapeDtypeStruct((m, n), F32),
        compiler_params=_params("arbitrary"),
        name="ada",
    )(c_all, w, b.reshape(1, n))


def _norm_mod(x, nw, shift, scale):
    y = x * lax.rsqrt(jnp.mean(x * x, axis=-1, keepdims=True) + EPS)
    y = y * nw
    return y * (1.0 + scale) + shift


HIST = SUBLANES
QKV_W = 2 * DN_DK + DN_DV
REST_PIECES = 5
REST_W = REST_PIECES * LANES


def _proj_kernel(x_ref, nw_ref, sh_ref, sc_ref, w_ref, wg_ref, cw_ref, conv0_ref,
                 pqkv_ref, prest_ref, pg_ref, convn_ref, h_ref, pe_ref, tail_ref,
                 *, seg, tiles_per_batch):
    i = pl.program_id(0)
    j = pl.program_id(1)
    tm = pqkv_ref.shape[0]
    nseg = tm // seg

    @pl.when(j == 0)
    def _():
        h = _norm_mod(x_ref[...], nw_ref[...], sh_ref[0], sc_ref[0])
        h_ref[...] = h.astype(BF16)
        pg_ref[...] = jnp.dot(h_ref[...], wg_ref[...], preferred_element_type=F32)

    acc = jnp.dot(h_ref[...], w_ref[...], preferred_element_type=F32)
    prest_ref[...] = acc[:, QKV_W:]

    for s in range(nseg):
        rows = acc[s * seg:(s + 1) * seg, 0:QKV_W]
        hist = jnp.concatenate([jnp.zeros((HIST - (CONV_W - 1), QKV_W), F32), conv0_ref[s]], axis=0)
        if tiles_per_batch > 1:
            hist = jnp.where(i % tiles_per_batch == 0, hist, tail_ref[j])
            tail_ref[j] = rows[seg - HIST:]
        pe_ref[s, 0:HIST, :] = hist
        pe_ref[s, HIST:HIST + seg, :] = rows
        convn_ref[s] = rows[seg - (CONV_W - 1):]
        out = pe_ref[s, pl.ds(HIST, seg), :] * cw_ref[CONV_W - 1:CONV_W, :]
        for t in range(1, CONV_W):
            out = out + pe_ref[s, pl.ds(HIST - t, seg), :] * cw_ref[CONV_W - 1 - t:CONV_W - t, :]
        out = _silu(out)
        q = out[:, 0:DN_DK]
        k = out[:, DN_DK:2 * DN_DK]
        q = q * lax.rsqrt(jnp.sum(q * q, axis=-1, keepdims=True) + EPS) * (DN_DK ** -0.5)
        k = k * lax.rsqrt(jnp.sum(k * k, axis=-1, keepdims=True) + EPS)
        pqkv_ref[s * seg:(s + 1) * seg, 0:DN_DK] = q
        pqkv_ref[s * seg:(s + 1) * seg, DN_DK:2 * DN_DK] = k
        pqkv_ref[s * seg:(s + 1) * seg, 2 * DN_DK:] = out[:, 2 * DN_DK:]


def _in_proj(x, nw, sh, sc, mod_map, w_main, w_gate, conv_w, conv0, t, tm):
    m, d = x.shape
    tn = QKV_W + REST_W
    nblk = w_main.shape[1] // tn
    r = sh.shape[1]
    assert conv_w.shape[1] == nblk * QKV_W
    seg = min(tm, t)
    assert tm % seg == 0 and t % seg == 0 and seg >= HIST
    nseg = tm // seg
    tiles_per_batch = t // seg
    assert nseg == 1 or tiles_per_batch == 1
    batch_blk = (lambda i: i // tiles_per_batch) if nseg == 1 else (lambda i: i)
    kern = functools.partial(_proj_kernel, seg=seg, tiles_per_batch=tiles_per_batch)
    p_qkv, p_rest, p_gate, conv_seg = pl.pallas_call(
        kern,
        grid=(m // tm, nblk),
        in_specs=[pl.BlockSpec((tm, d), lambda i, j: (i, 0)),
                  pl.BlockSpec((1, d), lambda i, j: (0, 0)),
                  pl.BlockSpec((1, r, d), lambda i, j: (mod_map(i, 0), 0, 0)),
                  pl.BlockSpec((1, r, d), lambda i, j: (mod_map(i, 1), 0, 0)),
                  pl.BlockSpec((d, tn), lambda i, j: (0, j)),
                  pl.BlockSpec((d, LANES), lambda i, j: (0, 0)),
                  pl.BlockSpec((CONV_W, QKV_W), lambda i, j: (0, j)),
                  pl.BlockSpec((nseg, CONV_W - 1, QKV_W), lambda i, j: (batch_blk(i), 0, j))],
        out_specs=[pl.BlockSpec((tm, QKV_W), lambda i, j: (i, j)),
                   pl.BlockSpec((tm, REST_W), lambda i, j: (i, j)),
                   pl.BlockSpec((tm, LANES), lambda i, j: (i, 0)),
                   pl.BlockSpec((nseg, CONV_W - 1, QKV_W), lambda i, j: (i, 0, j))],
        out_shape=[jax.ShapeDtypeStruct((m, nblk * QKV_W), F32),
                   jax.ShapeDtypeStruct((m, nblk * REST_W), F32),
                   jax.ShapeDtypeStruct((m, LANES), F32),
                   jax.ShapeDtypeStruct((m // seg,) + conv0.shape[1:], F32)],
        scratch_shapes=[pltpu.VMEM((tm, d), BF16),
                        pltpu.VMEM((nseg, HIST + seg, QKV_W), F32),
                        pltpu.VMEM((nblk, HIST, QKV_W), F32)],
        compiler_params=_params("arbitrary", "arbitrary"),
        name="in_proj",
    )(x, nw, sh, sc, w_main, w_gate, conv_w, conv0)
    conv_n = conv_seg.reshape((m // t, tiles_per_batch) + conv0.shape[1:])[:, -1]
    return p_qkv, p_rest, p_gate, conv_n


def _split(a):
    hi = a.astype(BF16)
    lo = (a - hi.astype(F32)).astype(BF16)
    return hi, lo


def _dot_split(a_parts, b_parts):
    ah, al = a_parts
    bh, bl = b_parts
    rows = ah.shape[0]
    top = jnp.dot(jnp.concatenate([ah, al], axis=0), bh, preferred_element_type=F32)
    return top[:rows] + top[rows:] + jnp.dot(ah, bl, preferred_element_type=F32)


def _neumann_correction(ms, fillers=()):
    c = ms[0].shape[0]
    fillers = list(fillers)

    def fill():
        if fillers:
            fillers.pop(0)()

    ns = [-m for m in ms]
    pws = [_dot_split(_split(m), _split(m)) for m in ms]
    fill()
    k = 2
    while k < c:
        pps = [_split(pw) for pw in pws]
        nps = [_split(n) for n in ns]
        if 2 * k < c:
            tops = [_dot_split((jnp.concatenate([pp[0], npart[0]], axis=0),
                                jnp.concatenate([pp[1], npart[1]], axis=0)), pp)
                    for pp, npart in zip(pps, nps)]
            fill()
            ns = [n + pw + top[c:] for n, pw, top in zip(ns, pws, tops)]
            pws = [top[:c] for top in tops]
        else:
            prods = [_dot_split(npart, pp) for npart, pp in zip(nps, pps)]
            fill()
            ns = [n + pw + prod for n, pw, prod in zip(ns, pws, prods)]
        k *= 2
    while fillers:
        fill()
    return ns


def _mixer_kernel(lg_ref, pqkv_ref, pg_ref, prest_ref, gp_ref, dnw_ref, inv_ref, dn0_ref, ret0_ref,
                  mix_ref, dn_ref, ret_ref,
                  m_s, qkg_s, rhs_s, qd_s, kd_s, cd_s,
                  *, chunk, nb, dn_heads, ret_heads, pos0):
    c = chunk
    n = pl.program_id(1)
    batches = range(nb)
    units = [(b, h) for b in batches for h in range(dn_heads)]
    runits = [(b, h) for b in batches for h in range(ret_heads)]
    uid = lambda u: u[0] * dn_heads + u[1]
    wr = n % 2
    rd = 1 - wr

    @pl.when(n == 0)
    def _():
        m_s[...] = jnp.zeros_like(m_s)
        qkg_s[...] = jnp.zeros_like(qkg_s)
        rhs_s[...] = jnp.zeros_like(rhs_s)
        qd_s[...] = jnp.zeros_like(qd_s)
        kd_s[...] = jnp.zeros_like(kd_s)
        cd_s[...] = jnp.zeros_like(cd_s)
        dn_ref[...] = dn0_ref[...]
        ret_ref[...] = ret0_ref[...]

    row = lax.broadcasted_iota(jnp.int32, (c, c), 0)
    col = lax.broadcasted_iota(jnp.int32, (c, c), 1)
    causal = row >= col
    strict = row > col

    mix_off = dn_heads * DN_DV

    beta_all, cs = [], []
    for b in batches:
        pg = pg_ref[b]
        beta_all.append(jax.nn.sigmoid(pg))
        g_all = gp_ref[0:1, :] * jax.nn.softplus(pg + gp_ref[1:2, :])
        g_hi, g_lo = _split(g_all)
        g_lo2 = (g_all - g_hi.astype(F32) - g_lo.astype(F32)).astype(BF16)
        cs.append(jnp.dot(causal.astype(BF16), jnp.concatenate([g_hi, g_lo, g_lo2], axis=-1),
                          preferred_element_type=F32))

    prep = {}

    def prep_values():
        for b in batches:
            decay_all = cs[b][:, :LANES] + cs[b][:, LANES:2 * LANES] + cs[b][:, 2 * LANES:]
            decay_t = decay_all.T
            for h in range(dn_heads):
                i = uid((b, h))
                q = pqkv_ref[b, :, h * QKV_W:h * QKV_W + DN_DK]
                k = pqkv_ref[b, :, h * QKV_W + DN_DK:h * QKV_W + 2 * DN_DK]
                v = pqkv_ref[b, :, h * QKV_W + 2 * DN_DK:(h + 1) * QKV_W]
                beta = beta_all[b][:, h:h + 1]
                dcol = decay_all[:, dn_heads + h:dn_heads + h + 1]
                drow = decay_t[dn_heads + h:dn_heads + h + 1, :]
                last = drow[:, c - 1:c]
                kb = k * beta
                edec = jnp.exp(dcol)
                prep[i] = dict(k=k, lhs=jnp.concatenate([kb, q], axis=0).astype(BF16),
                               gam=jnp.where(causal, jnp.exp(jnp.where(causal, dcol - drow, 0.0)), 0.0))
                rhs_s[wr, i] = jnp.concatenate([v * beta, kb * edec], axis=-1)
                qd_s[wr, i] = (q * edec).astype(BF16)
                kd_s[wr, i] = (k * jnp.exp(last - dcol)).astype(BF16)
                cd_s[wr, i] = jnp.broadcast_to(jnp.exp(last), (1, LANES))

    def prep_matmuls():
        for u in units:
            p = prep[uid(u)]
            p["both"] = _dot_nt(p["lhs"], p["k"])

    def prep_store():
        for u in units:
            i = uid(u)
            both, gam = prep[i]["both"], prep[i]["gam"]
            m_s[wr, i] = jnp.where(strict, both[:c] * gam, 0.0)
            qkg_s[wr, i] = (both[c:] * gam).astype(BF16)

    tok = lax.broadcasted_iota(jnp.int32, (c, 1), 0)
    idx = tok.astype(F32)
    ret = {}

    def piece(b, blk, kind):
        off = blk * REST_W + kind * LANES
        return prest_ref[b, :, off:off + LANES]

    def rot(b, h, kind, cos, sin):
        x1 = piece(b, 2 * h, kind)
        x2 = piece(b, 2 * h + 1, kind)
        return jnp.concatenate([x1 * cos - x2 * sin, x2 * cos + x1 * sin], axis=-1)

    def wide(b, h, kind):
        return jnp.concatenate([piece(b, 2 * h, kind), piece(b, 2 * h + 1, kind)], axis=-1)

    def ret_values():
        pos = (pos0 + (n - 1) * c + tok).astype(F32)
        ang = pos * inv_ref[...]
        cos = jnp.cos(ang)
        sin = jnp.sin(ang)
        for u in runits:
            b, h = u
            lg = lg_ref[h]
            rq = rot(b, h, 1, cos, sin)
            rk = rot(b, h, 2, cos, sin) * (RET_DK ** -0.5)
            ret[u] = dict(rq=rq.astype(BF16), rk=rk.astype(BF16),
                          rv=wide(b, h, 3).astype(BF16),
                          qx=(rq * jnp.exp((idx + 1.0) * lg)).astype(BF16),
                          kz=(rk * jnp.exp((c - 1.0 - idx) * lg)).astype(BF16))

    def ret_scores():
        for u in runits:
            ret[u]["qk"] = _dot_nt(ret[u]["rq"], ret[u]["rk"])

    def ret_inner():
        dpos = (row - col).astype(F32)
        dmask = [jnp.where(causal, jnp.exp(jnp.where(causal, dpos, 0.0) * lg_ref[h]), 0.0)
                 for h in range(ret_heads)]
        for u in runits:
            ret[u]["inner"] = jnp.dot((ret[u]["qk"] * dmask[u[1]]).astype(BF16), ret[u]["rv"],
                                      preferred_element_type=F32)

    ns = _neumann_correction([m_s[rd, uid(u)] for u in units],
                             [prep_values, prep_matmuls, prep_store, ret_values, ret_scores, ret_inner])
    rhs = [rhs_s[rd, uid(u)] for u in units]
    sols = [r + _dot_split(_split(nn), _split(r)) for r, nn in zip(rhs, ns)]

    sd = [dn_ref[b, h] for b, h in units]
    ws = [jnp.dot(jnp.concatenate([sol[:, DN_DV:].astype(BF16), qd_s[rd, uid(u)]], axis=0),
                  s.astype(BF16), preferred_element_type=F32) for u, sol, s in zip(units, sols, sd)]
    sr = [ret_ref[b, h] for b, h in runits]
    ro = [ret[u]["inner"] + jnp.dot(ret[u]["qx"], s.astype(BF16), preferred_element_type=F32)
          for u, s in zip(runits, sr)]
    rupd = [_dot_tn(ret[u]["kz"], ret[u]["rv"]) for u in runits]
    us = [(sol[:, :DN_DV] - w[:c]).astype(BF16) for sol, w in zip(sols, ws)]
    o2 = [jnp.dot(qkg_s[rd, uid(u)], uu, preferred_element_type=F32) for u, uu in zip(units, us)]
    upd = [_dot_tn(kd_s[rd, uid(u)], uu) for u, uu in zip(units, us)]
    cdr = [jnp.exp(jnp.zeros((1, 1), F32) + c * lg_ref[h]) for h in range(ret_heads)]
    for k, (b, h) in enumerate(runits):
        ret_ref[b, h] = sr[k] * cdr[h] + rupd[k]
        o = ro[k]
        o = o * lax.rsqrt(jnp.mean(o * o, axis=-1, keepdims=True) + EPS)
        gg = wide(b, h, 4)
        mix_ref[b, :, mix_off + h * RET_DV:mix_off + (h + 1) * RET_DV] = (
            (o * _silu(gg)).astype(mix_ref.dtype))
    for k, (b, h) in enumerate(units):
        dn_ref[b, h] = sd[k] * cd_s[rd, uid((b, h))][:, 0:1] + upd[k]
        o = ws[k][c:] + o2[k]
        o = o * lax.rsqrt(jnp.mean(o * o, axis=-1, keepdims=True) + EPS) * dnw_ref[...]
        zz = piece(b, h, 0)
        mix_ref[b, :, h * DN_DV:(h + 1) * DN_DV] = (o * _silu(zz)).astype(mix_ref.dtype)

    @pl.when(n == 0)
    def _():
        dn_ref[...] = dn0_ref[...]
        ret_ref[...] = ret0_ref[...]


def _mixer(p_qkv, p_rest, p_gate, gate_par, dn_norm, inv_freq, log_gamma, dn0, ret0,
           *, batch, chunk, pos0):
    m = p_qkv.shape[0]
    t = m // batch
    nchunk = t // chunk
    dn_heads = dn0.shape[1]
    ret_heads = ret0.shape[1]
    mix_w = dn_heads * DN_DV + ret_heads * RET_DV
    assert p_qkv.shape[1] == dn_heads * QKV_W and p_rest.shape[1] == dn_heads * REST_W
    assert dn_heads == 2 * ret_heads and RET_DK == 2 * LANES and RET_DV == 2 * LANES
    nb = 2 if batch % 2 == 0 else 1
    kern = functools.partial(_mixer_kernel, chunk=chunk, nb=nb, dn_heads=dn_heads,
                             ret_heads=ret_heads, pos0=pos0)
    full = lambda shape: pl.BlockSpec(shape, lambda g, n: (0,) * len(shape))
    cur = lambda n: jnp.minimum(n, nchunk - 1)
    prev = lambda n: jnp.maximum(n - 1, 0)
    state = lambda a: pl.BlockSpec((nb,) + a.shape[1:], lambda g, n: (g,) + (0,) * (a.ndim - 1))
    units = nb * dn_heads
    mixed, dn_n, ret_n = pl.pallas_call(
        kern,
        grid=(batch // nb, nchunk + 1),
        in_specs=[pl.BlockSpec(memory_space=pltpu.SMEM),
                  pl.BlockSpec((nb, chunk, p_qkv.shape[1]), lambda g, n: (g, cur(n), 0)),
                  pl.BlockSpec((nb, chunk, LANES), lambda g, n: (g, cur(n), 0)),
                  pl.BlockSpec((nb, chunk, p_rest.shape[1]), lambda g, n: (g, prev(n), 0)),
                  full(gate_par.shape), full(dn_norm.shape), full(inv_freq.shape),
                  state(dn0), state(ret0)],
        out_specs=[pl.BlockSpec((nb, chunk, mix_w), lambda g, n: (g, prev(n), 0)),
                   state(dn0), state(ret0)],
        out_shape=[jax.ShapeDtypeStruct((batch, t, mix_w), BF16),
                   jax.ShapeDtypeStruct(dn0.shape, F32),
                   jax.ShapeDtypeStruct(ret0.shape, F32)],
        scratch_shapes=[pltpu.VMEM((2, units, chunk, chunk), F32),
                        pltpu.VMEM((2, units, chunk, chunk), BF16),
                        pltpu.VMEM((2, units, chunk, DN_DV + DN_DK), F32),
                        pltpu.VMEM((2, units, chunk, DN_DK), BF16),
                        pltpu.VMEM((2, units, chunk, DN_DK), BF16),
                        pltpu.VMEM((2, units, 1, LANES), F32)],
        compiler_params=_params("parallel", "arbitrary"),
        name="mixer",
    )(log_gamma, p_qkv.reshape(batch, t, -1), p_gate.reshape(batch, t, LANES),
      p_rest.reshape(batch, t, -1), gate_par, dn_norm, inv_freq, dn0, ret0)
    return mixed.reshape(m, mix_w), dn_n, ret_n


def _ffn_kernel(x_ref, mix_ref, gm_ref, wo_ref, nw_ref, sh_ref, sc_ref, gf_ref, wg_ref, wu_ref, wd_ref,
                nf_ref, shf_ref, scf_ref, y_ref, x1_ref, h_ref):
    j = pl.program_id(1)

    @pl.when(j == 0)
    def _():
        x1 = x_ref[...] + gm_ref[0] * jnp.dot(mix_ref[...], wo_ref[...], preferred_element_type=F32)
        x1_ref[...] = x1
        h = _norm_mod(x1, nw_ref[...], sh_ref[0], sc_ref[0])
        h_ref[...] = h.astype(BF16)
        y_ref[...] = jnp.zeros_like(y_ref)

    hb = h_ref[...]
    gate = jnp.dot(hb, wg_ref[...], preferred_element_type=F32)
    up = jnp.dot(hb, wu_ref[...], preferred_element_type=F32)
    act = (_silu(gate) * up).astype(BF16)
    y_ref[...] += jnp.dot(act, wd_ref[...], preferred_element_type=F32)

    @pl.when(j == pl.num_programs(1) - 1)
    def _():
        x2 = x1_ref[...] + gf_ref[0] * y_ref[...]
        y_ref[...] = _norm_mod(x2, nf_ref[...], shf_ref[0], scf_ref[0])


def _ffn(x, mixed, w_out, nw, ada, mod_map, w_gu, w_down, nf, ada_fin, fin_map, tm, tf):
    m, d = x.shape
    kmix = mixed.shape[1]
    dff = w_down.shape[0]
    nf_blocks = dff // tf
    r = ada.shape[1]
    mod = lambda k: pl.BlockSpec((1, r, d), lambda i, j: (mod_map(i, k), 0, 0))
    fin = lambda k: pl.BlockSpec((1, r, d), lambda i, j: (fin_map(i, k), 0, 0))
    return pl.pallas_call(
        _ffn_kernel,
        grid=(m // tm, nf_blocks),
        in_specs=[pl.BlockSpec((tm, d), lambda i, j: (i, 0)),
                  pl.BlockSpec((tm, kmix), lambda i, j: (i, 0)),
                  mod(2),
                  pl.BlockSpec((kmix, d), lambda i, j: (0, 0), pipeline_mode=pl.Buffered(1)),
                  pl.BlockSpec((1, d), lambda i, j: (0, 0)),
                  mod(3), mod(4), mod(5),
                  pl.BlockSpec((d, tf), lambda i, j: (0, j)),
                  pl.BlockSpec((d, tf), lambda i, j: (0, j + nf_blocks)),
                  pl.BlockSpec((tf, d), lambda i, j: (j, 0)),
                  pl.BlockSpec((1, d), lambda i, j: (0, 0)),
                  fin(0), fin(1)],
        out_specs=pl.BlockSpec((tm, d), lambda i, j: (i, 0)),
        out_shape=jax.ShapeDtypeStruct((m, d), F32),
        scratch_shapes=[pltpu.VMEM((tm, d), F32), pltpu.VMEM((tm, d), BF16)],
        compiler_params=_params("parallel", "arbitrary"),
        name="ffn",
    )(x, mixed, ada, w_out, nw, ada, ada, ada, w_gu, w_gu, w_down, nf, ada_fin, ada_fin)


def _deinterleave(a, groups, axis):
    shp = a.shape
    per = shp[axis] // groups
    a = a.reshape(shp[:axis] + (groups, per // 2, 2) + shp[axis + 1:])
    return jnp.swapaxes(a, axis + 1, axis + 2).reshape(shp)


def _interleave(a, axis):
    shp = a.shape
    a = a.reshape(shp[:axis] + (2, shp[axis] // 2) + shp[axis + 1:])
    return jnp.swapaxes(a, axis, axis + 1).reshape(shp)


def _to_blocks(a, heads):
    shp = a.shape
    kinds = shp[-1] // (heads * LANES)
    a = a.reshape(shp[:-1] + (kinds, heads, LANES))
    return jnp.swapaxes(a, -3, -2).reshape(shp)


def _from_blocks(a, heads):
    shp = a.shape
    kinds = shp[-1] // (heads * LANES)
    a = a.reshape(shp[:-1] + (heads, kinds, LANES))
    return jnp.swapaxes(a, -3, -2).reshape(shp)


class _Tiles(NamedTuple):
    proj_rows: int
    ffn_rows: int
    ffn_cols: int


def _stream(x, ada, ada_fin, conv0, dn0, ret0, weights, *, chunk, pos0, tiles, per_row):
    (nw_mix, nw_ffn, w_main, w_gate, conv_w, gate_par, dn_norm, inv_freq, log_gamma,
     w_out, w_gu, w_down, nf) = weights
    b, t, d = x.shape
    m = b * t
    xf = x.reshape(m, d)
    if per_row:
        ada_r = jnp.repeat(ada.reshape(b, N_ADA, d), t, axis=0).transpose(1, 0, 2)
        fin_r = jnp.repeat(ada_fin.reshape(b, 2, d), t, axis=0).transpose(1, 0, 2)
        mod_map = lambda tm: (lambda i, k: k)
        fin_map = lambda tm: (lambda i, k: k)
    else:
        assert t % tiles.proj_rows == 0 and t % tiles.ffn_rows == 0
        ada_r = ada.reshape(b * N_ADA, 1, d)
        fin_r = ada_fin.reshape(b * 2, 1, d)
        mod_map = lambda tm: (lambda i, k: (i // (t // tm)) * N_ADA + k)
        fin_map = lambda tm: (lambda i, k: (i // (t // tm)) * 2 + k)

    p_qkv, p_rest, p_gate, conv_n = _in_proj(xf, nw_mix, ada_r, ada_r, mod_map(tiles.proj_rows),
                                             w_main, w_gate, conv_w, conv0, t, tiles.proj_rows)
    mixed, dn_n, ret_n = _mixer(p_qkv, p_rest, p_gate, gate_par, dn_norm, inv_freq, log_gamma,
                                dn0, ret0, batch=b, chunk=chunk, pos0=pos0)
    y = _ffn(xf, mixed, w_out, nw_ffn, ada_r, mod_map(tiles.ffn_rows), w_gu, w_down, nf, fin_r,
             fin_map(tiles.ffn_rows), tiles.ffn_rows, tiles.ffn_cols)
    return y.reshape(b, t, d), conv_n, dn_n, ret_n


def kernel(x_prompt, x_sample, state_conv, state_delta, state_ret, c_prompt, c_sample, norm_mix, norm_ffn, w_ada, b_ada, w_in, conv_w, dn_a_log, dn_dt_bias, dn_norm, w_out, w_gu, w_down, norm_final, w_ada_final, b_ada_final):
    bp, tp, d = x_prompt.shape
    bs, ts, _ = x_sample.shape
    depth = w_in.shape[0]
    assert depth == 1, "single-layer pipeline"
    dn_heads = state_delta.shape[2]
    ret_heads = state_ret.shape[2]
    conv_ch = conv_w.shape[2]
    assert 2 * dn_heads <= LANES

    nb = bp + bs
    pad = (-nb) % (2 * SUBLANES)
    c_all = jnp.concatenate([c_prompt, c_sample, jnp.zeros((pad, d), F32)], axis=0)
    ada_all = _ada(c_all, w_ada[0], b_ada[0], 1024)
    fin_all = _ada(c_all, w_ada_final, b_ada_final, 1024)

    l = 0
    g_off = conv_ch + dn_heads * DN_DV
    r_off = g_off + 2 * dn_heads
    rw = ret_heads * RET_DK
    w_l = w_in[l]
    cols = np.arange(w_l.shape[1])
    deint = lambda c: c.reshape(ret_heads, RET_DK // 2, 2).swapaxes(1, 2).reshape(-1)
    kinds = np.concatenate([cols[:g_off], deint(cols[r_off:r_off + rw]),
                            deint(cols[r_off + rw:r_off + 2 * rw]), cols[r_off + 2 * rw:]])
    col_idx = kinds.reshape(-1, dn_heads, LANES).swapaxes(0, 1).reshape(-1)
    w_main = jnp.take(w_l, col_idx, axis=1).astype(BF16)
    w_gate = jnp.concatenate([w_l[:, g_off:r_off],
                              jnp.zeros((d, LANES - 2 * dn_heads), F32)], axis=1).astype(BF16)
    lane_pad = jnp.zeros((LANES - 2 * dn_heads,), F32)
    gate_par = jnp.stack([
        jnp.concatenate([jnp.zeros((dn_heads,), F32), -jnp.exp(dn_a_log[l].astype(F32)), lane_pad]),
        jnp.concatenate([jnp.zeros((dn_heads,), F32), dn_dt_bias[l].astype(F32), lane_pad])])
    inv_freq = (1.0 / (ROPE_BASE ** jnp.linspace(0.0, 1.0, RET_DK // 2, dtype=F32))).reshape(1, -1)
    log_gamma = jnp.log(1.0 - 2.0 ** (-5.0 - jnp.arange(ret_heads, dtype=F32)))
    weights = (norm_mix[l].reshape(1, d), norm_ffn[l].reshape(1, d), w_main, w_gate,
               _to_blocks(conv_w[l], dn_heads),
               gate_par, dn_norm[l].reshape(1, -1), inv_freq, log_gamma,
               w_out[l].astype(BF16), w_gu[l].astype(BF16), w_down[l].astype(BF16),
               norm_final.reshape(1, d))

    zc = jnp.zeros((bp, CONV_W - 1, conv_ch), F32)
    zd = jnp.zeros((bp,) + state_delta.shape[2:], F32)
    zr = jnp.zeros((bp,) + state_ret.shape[2:], F32)

    y_p, cp, dp, rp = _stream(x_prompt, ada_all[:bp], fin_all[:bp], zc, zd, zr, weights,
                              chunk=CHUNK, pos0=0, per_row=False,
                              tiles=_Tiles(proj_rows=1024, ffn_rows=512, ffn_cols=512))
    rs_in = _deinterleave(state_ret[l], 1, axis=2)
    y_s, cs, ds, rs = _stream(x_sample, ada_all[bp:nb], fin_all[bp:nb],
                              _to_blocks(state_conv[l], dn_heads), state_delta[l], rs_in, weights,
                              chunk=ts, pos0=PAST_LEN, per_row=True,
                              tiles=_Tiles(proj_rows=bs * ts, ffn_rows=bs * ts, ffn_cols=512))
    cp, cs = _from_blocks(cp, dn_heads), _from_blocks(cs, dn_heads)
    rp, rs = _interleave(rp, axis=2), _interleave(rs, axis=2)
    return (y_p, y_s, cp[None], dp[None], rp[None], cs[None], ds[None], rs[None])
```

```python
import functools
from typing import NamedTuple

import jax
import jax.numpy as jnp
from jax import lax
from jax.experimental import pallas as pl
from jax.experimental.pallas import tpu as pltpu

F32 = jnp.float32
BF16 = jnp.bfloat16

DN_DK = 128
DN_DV = 128
RET_DK = 256
RET_DV = 256
CONV_W = 4
N_ADA = 6
CHUNK = 64
PAST_LEN = 2048
ROPE_BASE = 10000.0
EPS = 1e-6

LANES = 128
SUBLANES = 8
VMEM_LIMIT = 56 * 1024 * 1024


def _silu(x):
    return x * jax.nn.sigmoid(x)


def _dot(a, b):
    return jnp.dot(a.astype(BF16), b.astype(BF16), preferred_element_type=F32)


def _dot_nt(a, b):
    return lax.dot_general(a.astype(BF16), b.astype(BF16), (((1,), (1,)), ((), ())),
                           preferred_element_type=F32)


def _dot_tn(a, b):
    return lax.dot_general(a.astype(BF16), b.astype(BF16), (((0,), (0,)), ((), ())),
                           preferred_element_type=F32)


def _params(*sem):
    return pltpu.CompilerParams(dimension_semantics=sem, vmem_limit_bytes=VMEM_LIMIT)


def _ada_kernel(c_ref, w_ref, b_ref, o_ref):
    s = _silu(c_ref[...])
    o_ref[...] = _dot(s, w_ref[...]) + b_ref[...]


def _ada(c_all, w, b, tn):
    m, d = c_all.shape
    n = w.shape[1]
    return pl.pallas_call(
        _ada_kernel,
        grid=(n // tn,),
        in_specs=[pl.BlockSpec((m, d), lambda j: (0, 0)),
                  pl.BlockSpec((d, tn), lambda j: (0, j)),
                  pl.BlockSpec((1, tn), lambda j: (0, j))],
        out_specs=pl.BlockSpec((m, tn), lambda j: (0, j)),
        out_shape=jax.ShapeDtypeStruct((m, n), F32),
        compiler_params=_params("arbitrary"),
        name="ada",
    )(c_all, w, b.reshape(1, n))


def _norm_mod(x, nw, shift, scale):
    y = x * lax.rsqrt(jnp.mean(x * x, axis=-1, keepdims=True) + EPS)
    y = y * nw
    return y * (1.0 + scale) + shift


HIST = SUBLANES
QKV_W = 2 * DN_DK + DN_DV
REST_PIECES = 5
REST_W = REST_PIECES * LANES


def _proj_kernel(x_ref, nw_ref, sh_ref, sc_ref, w_ref, wg_ref, cw_ref, conv0_ref,
                 pqkv_ref, prest_ref, pg_ref, convn_ref, h_ref, pe_ref, tail_ref,
                 *, seg, tiles_per_batch):
    i = pl.program_id(0)
    j = pl.program_id(1)
    tm = pqkv_ref.shape[0]
    nseg = tm // seg

    @pl.when(j == 0)
    def _():
        h = _norm_mod(x_ref[...], nw_ref[...], sh_ref[0], sc_ref[0])
        h_ref[...] = h.astype(BF16)
        pg_ref[...] = jnp.dot(h_ref[...], wg_ref[...], preferred_element_type=F32)

    acc = jnp.dot(h_ref[...], w_ref[...], preferred_element_type=F32)
    prest_ref[...] = acc[:, QKV_W:]

    for s in range(nseg):
        rows = acc[s * seg:(s + 1) * seg, 0:QKV_W]
        hist = jnp.concatenate([jnp.zeros((HIST - (CONV_W - 1), QKV_W), F32), conv0_ref[s]], axis=0)
        if tiles_per_batch > 1:
            hist = jnp.where(i % tiles_per_batch == 0, hist, tail_ref[j])
            tail_ref[j] = rows[seg - HIST:]
        pe_ref[s, 0:HIST, :] = hist
        pe_ref[s, HIST:HIST + seg, :] = rows
        convn_ref[s] = rows[seg - (CONV_W - 1):]
        out = pe_ref[s, pl.ds(HIST, seg), :] * cw_ref[CONV_W - 1:CONV_W, :]
        for t in range(1, CONV_W):
            out = out + pe_ref[s, pl.ds(HIST - t, seg), :] * cw_ref[CONV_W - 1 - t:CONV_W - t, :]
        out = _silu(out)
        q = out[:, 0:DN_DK]
        k = out[:, DN_DK:2 * DN_DK]
        q = q * lax.rsqrt(jnp.sum(q * q, axis=-1, keepdims=True) + EPS) * (DN_DK ** -0.5)
        k = k * lax.rsqrt(jnp.sum(k * k, axis=-1, keepdims=True) + EPS)
        pqkv_ref[s * seg:(s + 1) * seg, 0:DN_DK] = q
        pqkv_ref[s * seg:(s + 1) * seg, DN_DK:2 * DN_DK] = k
        pqkv_ref[s * seg:(s + 1) * seg, 2 * DN_DK:] = out[:, 2 * DN_DK:]


def _in_proj(x, nw, sh, sc, mod_map, w_main, w_gate, conv_w, conv0, t, tm):
    m, d = x.shape
    tn = QKV_W + REST_W
    nblk = w_main.shape[1] // tn
    r = sh.shape[1]
    assert conv_w.shape[1] == nblk * QKV_W
    seg = min(tm, t)
    assert tm % seg == 0 and t % seg == 0 and seg >= HIST
    nseg = tm // seg
    tiles_per_batch = t // seg
    assert nseg == 1 or tiles_per_batch == 1
    batch_blk = (lambda i: i // tiles_per_batch) if nseg == 1 else (lambda i: i)
    kern = functools.partial(_proj_kernel, seg=seg, tiles_per_batch=tiles_per_batch)
    p_qkv, p_rest, p_gate, conv_seg = pl.pallas_call(
        kern,
        grid=(m // tm, nblk),
        in_specs=[pl.BlockSpec((tm, d), lambda i, j: (i, 0)),
                  pl.BlockSpec((1, d), lambda i, j: (0, 0)),
                  pl.BlockSpec((1, r, d), lambda i, j: (mod_map(i, 0), 0, 0)),
                  pl.BlockSpec((1, r, d), lambda i, j: (mod_map(i, 1), 0, 0)),
                  pl.BlockSpec((d, tn), lambda i, j: (0, j)),
                  pl.BlockSpec((d, LANES), lambda i, j: (0, 0)),
                  pl.BlockSpec((CONV_W, QKV_W), lambda i, j: (0, j)),
                  pl.BlockSpec((nseg, CONV_W - 1, QKV_W), lambda i, j: (batch_blk(i), 0, j))],
        out_specs=[pl.BlockSpec((tm, QKV_W), lambda i, j: (i, j)),
                   pl.BlockSpec((tm, REST_W), lambda i, j: (i, j)),
                   pl.BlockSpec((tm, LANES), lambda i, j: (i, 0)),
                   pl.BlockSpec((nseg, CONV_W - 1, QKV_W), lambda i, j: (i, 0, j))],
        out_shape=[jax.ShapeDtypeStruct((m, nblk * QKV_W), F32),
                   jax.ShapeDtypeStruct((m, nblk * REST_W), F32),
                   jax.ShapeDtypeStruct((m, LANES), F32),
                   jax.ShapeDtypeStruct((m // seg,) + conv0.shape[1:], F32)],
        scratch_shapes=[pltpu.VMEM((tm, d), BF16),
                        pltpu.VMEM((nseg, HIST + seg, QKV_W), F32),
                        pltpu.VMEM((nblk, HIST, QKV_W), F32)],
        compiler_params=_params("arbitrary", "arbitrary"),
        name="in_proj",
    )(x, nw, sh, sc, w_main, w_gate, conv_w, conv0)
    conv_n = conv_seg.reshape((m // t, tiles_per_batch) + conv0.shape[1:])[:, -1]
    return p_qkv, p_rest, p_gate, conv_n


def _split(a):
    hi = a.astype(BF16)
    lo = (a - hi.astype(F32)).astype(BF16)
    return hi, lo


def _dot_split(a_parts, b_parts):
    ah, al = a_parts
    bh, bl = b_parts
    rows = ah.shape[0]
    top = jnp.dot(jnp.concatenate([ah, al], axis=0), bh, preferred_element_type=F32)
    return top[:rows] + top[rows:] + jnp.dot(ah, bl, preferred_element_type=F32)


def _neumann_correction(ms, fillers=()):
    c = ms[0].shape[0]
    fillers = list(fillers)

    def fill():
        if fillers:
            fillers.pop(0)()

    ns = [-m for m in ms]
    pws = [_dot_split(_split(m), _split(m)) for m in ms]
    fill()
    k = 2
    while k < c:
        pps = [_split(pw) for pw in pws]
        nps = [_split(n) for n in ns]
        if 2 * k < c:
            tops = [_dot_split((jnp.concatenate([pp[0], npart[0]], axis=0),
                                jnp.concatenate([pp[1], npart[1]], axis=0)), pp)
                    for pp, npart in zip(pps, nps)]
            fill()
            ns = [n + pw + top[c:] for n, pw, top in zip(ns, pws, tops)]
            pws = [top[:c] for top in tops]
        else:
            prods = [_dot_split(npart, pp) for npart, pp in zip(nps, pps)]
            fill()
            ns = [n + pw + prod for n, pw, prod in zip(ns, pws, prods)]
        k *= 2
    while fillers:
        fill()
    return ns


def _mixer_kernel(lg_ref, pqkv_ref, pg_ref, prest_ref, gp_ref, dnw_ref, inv_ref, dn0_ref, ret0_ref,
                  mix_ref, dn_ref, ret_ref,
                  m_s, qkg_s, rhs_s, qd_s, kd_s, cd_s,
                  *, chunk, nb, dn_heads, ret_heads, pos0):
    c = chunk
    n = pl.program_id(1)
    batches = range(nb)
    units = [(b, h) for b in batches for h in range(dn_heads)]
    runits = [(b, h) for b in batches for h in range(ret_heads)]
    uid = lambda u: u[0] * dn_heads + u[1]
    wr = n % 2
    rd = 1 - wr

    @pl.when(n == 0)
    def _():
        m_s[...] = jnp.zeros_like(m_s)
        qkg_s[...] = jnp.zeros_like(qkg_s)
        rhs_s[...] = jnp.zeros_like(rhs_s)
        qd_s[...] = jnp.zeros_like(qd_s)
        kd_s[...] = jnp.zeros_like(kd_s)
        cd_s[...] = jnp.zeros_like(cd_s)
        dn_ref[...] = dn0_ref[...]
        ret_ref[...] = ret0_ref[...]

    row = lax.broadcasted_iota(jnp.int32, (c, c), 0)
    col = lax.broadcasted_iota(jnp.int32, (c, c), 1)
    causal = row >= col
    strict = row > col

    mix_off = dn_heads * DN_DV

    beta_all, cs = [], []
    for b in batches:
        pg = pg_ref[b]
        beta_all.append(jax.nn.sigmoid(pg))
        g_all = gp_ref[0:1, :] * jax.nn.softplus(pg + gp_ref[1:2, :])
        g_hi, g_lo = _split(g_all)
        g_lo2 = (g_all - g_hi.astype(F32) - g_lo.astype(F32)).astype(BF16)
        cs.append(jnp.dot(causal.astype(BF16), jnp.concatenate([g_hi, g_lo, g_lo2], axis=-1),
                          preferred_element_type=F32))

    prep = {}

    def prep_values():
        for b in batches:
            decay_all = cs[b][:, :LANES] + cs[b][:, LANES:2 * LANES] + cs[b][:, 2 * LANES:]
            decay_t = decay_all.T
            for h in range(dn_heads):
                i = uid((b, h))
                q = pqkv_ref[b, :, h * QKV_W:h * QKV_W + DN_DK]
                k = pqkv_ref[b, :, h * QKV_W + DN_DK:h * QKV_W + 2 * DN_DK]
                v = pqkv_ref[b, :, h * QKV_W + 2 * DN_DK:(h + 1) * QKV_W]
                beta = beta_all[b][:, h:h + 1]
                dcol = decay_all[:, dn_heads + h:dn_heads + h + 1]
                drow = decay_t[dn_heads + h:dn_heads + h + 1, :]
                last = drow[:, c - 1:c]
                kb = k * beta
                edec = jnp.exp(dcol)
                prep[i] = dict(k=k, lhs=jnp.concatenate([kb, q], axis=0).astype(BF16),
                               gam=jnp.where(causal, jnp.exp(jnp.where(causal, dcol - drow, 0.0)), 0.0))
                rhs_s[wr, i] = jnp.concatenate([v * beta, kb * edec], axis=-1)
                qd_s[wr, i] = (q * edec).astype(BF16)
                kd_s[wr, i] = (k * jnp.exp(last - dcol)).astype(BF16)
                cd_s[wr, i] = jnp.broadcast_to(jnp.exp(last), (1, LANES))

    def prep_matmuls():
        for u in units:
            p = prep[uid(u)]
            p["both"] = _dot_nt(p["lhs"], p["k"])

    def prep_store():
        for u in units:
            i = uid(u)
            both, gam = prep[i]["both"], prep[i]["gam"]
            m_s[wr, i] = jnp.where(strict, both[:c] * gam, 0.0)
            qkg_s[wr, i] = (both[c:] * gam).astype(BF16)

    tok = lax.broadcasted_iota(jnp.int32, (c, 1), 0)
    idx = tok.astype(F32)
    ret = {}

    def piece(b, blk, kind):
        off = blk * REST_W + kind * LANES
        return prest_ref[b, :, off:off + LANES]

    even_lane = lax.broadcasted_iota(jnp.int32, (c, LANES), 1) % 2 == 0

    def rot(b, h, kind, cos, sin):
        halves = []
        for p in range(2):
            x = piece(b, 2 * h + p, kind)
            partner = jnp.where(even_lane, pltpu.roll(x, LANES - 1, axis=1), pltpu.roll(x, 1, axis=1))
            lanes = slice(p * LANES, (p + 1) * LANES)
            halves.append(x * cos[:, lanes] + partner * sin[:, lanes])
        return jnp.concatenate(halves, axis=-1)

    def wide(b, h, kind):
        return jnp.concatenate([piece(b, 2 * h, kind), piece(b, 2 * h + 1, kind)], axis=-1)

    def ret_values():
        pos = (pos0 + (n - 1) * c + tok).astype(F32)
        ang = pos * inv_ref[...]
        cos = jnp.cos(ang)
        sin = jnp.sin(ang)
        for u in runits:
            b, h = u
            lg = lg_ref[h]
            rq = rot(b, h, 1, cos, sin)
            rk = rot(b, h, 2, cos, sin) * (RET_DK ** -0.5)
            ret[u] = dict(rq=rq.astype(BF16), rk=rk.astype(BF16),
                          rv=wide(b, h, 3).astype(BF16),
                          qx=(rq * jnp.exp((idx + 1.0) * lg)).astype(BF16),
                          kz=(rk * jnp.exp((c - 1.0 - idx) * lg)).astype(BF16))

    def ret_scores():
        for u in runits:
            ret[u]["qk"] = _dot_nt(ret[u]["rq"], ret[u]["rk"])

    def ret_inner():
        dpos = (row - col).astype(F32)
        dmask = [jnp.where(causal, jnp.exp(jnp.where(causal, dpos, 0.0) * lg_ref[h]), 0.0)
                 for h in range(ret_heads)]
        for u in runits:
            ret[u]["inner"] = jnp.dot((ret[u]["qk"] * dmask[u[1]]).astype(BF16), ret[u]["rv"],
                                      preferred_element_type=F32)

    ns = _neumann_correction([m_s[rd, uid(u)] for u in units],
                             [prep_values, prep_matmuls, prep_store, ret_values, ret_scores, ret_inner])
    rhs = [rhs_s[rd, uid(u)] for u in units]
    sols = [r + _dot_split(_split(nn), _split(r)) for r, nn in zip(rhs, ns)]

    sd = [dn_ref[b, h] for b, h in units]
    ws = [jnp.dot(jnp.concatenate([sol[:, DN_DV:].astype(BF16), qd_s[rd, uid(u)]], axis=0),
                  s.astype(BF16), preferred_element_type=F32) for u, sol, s in zip(units, sols, sd)]
    sr = [ret_ref[b, h] for b, h in runits]
    ro = [ret[u]["inner"] + jnp.dot(ret[u]["qx"], s.astype(BF16), preferred_element_type=F32)
          for u, s in zip(runits, sr)]
    rupd = [_dot_tn(ret[u]["kz"], ret[u]["rv"]) for u in runits]
    us = [(sol[:, :DN_DV] - w[:c]).astype(BF16) for sol, w in zip(sols, ws)]
    o2 = [jnp.dot(qkg_s[rd, uid(u)], uu, preferred_element_type=F32) for u, uu in zip(units, us)]
    upd = [_dot_tn(kd_s[rd, uid(u)], uu) for u, uu in zip(units, us)]
    cdr = [jnp.exp(jnp.zeros((1, 1), F32) + c * lg_ref[h]) for h in range(ret_heads)]
    for k, (b, h) in enumerate(runits):
        ret_ref[b, h] = sr[k] * cdr[h] + rupd[k]
        o = ro[k]
        o = o * lax.rsqrt(jnp.mean(o * o, axis=-1, keepdims=True) + EPS)
        gg = wide(b, h, 4)
        mix_ref[b, :, mix_off + h * RET_DV:mix_off + (h + 1) * RET_DV] = (
            (o * _silu(gg)).astype(mix_ref.dtype))
    for k, (b, h) in enumerate(units):
        dn_ref[b, h] = sd[k] * cd_s[rd, uid((b, h))][:, 0:1] + upd[k]
        o = ws[k][c:] + o2[k]
        o = o * lax.rsqrt(jnp.mean(o * o, axis=-1, keepdims=True) + EPS) * dnw_ref[...]
        zz = piece(b, h, 0)
        mix_ref[b, :, h * DN_DV:(h + 1) * DN_DV] = (o * _silu(zz)).astype(mix_ref.dtype)

    @pl.when(n == 0)
    def _():
        dn_ref[...] = dn0_ref[...]
        ret_ref[...] = ret0_ref[...]


def _mixer(p_qkv, p_rest, p_gate, gate_par, dn_norm, inv_freq, log_gamma, dn0, ret0,
           *, batch, chunk, pos0):
    m = p_qkv.shape[0]
    t = m // batch
    nchunk = t // chunk
    dn_heads = dn0.shape[1]
    ret_heads = ret0.shape[1]
    mix_w = dn_heads * DN_DV + ret_heads * RET_DV
    assert p_qkv.shape[1] == dn_heads * QKV_W and p_rest.shape[1] == dn_heads * REST_W
    assert dn_heads == 2 * ret_heads and RET_DK == 2 * LANES and RET_DV == 2 * LANES
    nb = 2 if batch % 2 == 0 else 1
    kern = functools.partial(_mixer_kernel, chunk=chunk, nb=nb, dn_heads=dn_heads,
                             ret_heads=ret_heads, pos0=pos0)
    full = lambda shape: pl.BlockSpec(shape, lambda g, n: (0,) * len(shape))
    cur = lambda n: jnp.minimum(n, nchunk - 1)
    prev = lambda n: jnp.maximum(n - 1, 0)
    state = lambda a: pl.BlockSpec((nb,) + a.shape[1:], lambda g, n: (g,) + (0,) * (a.ndim - 1))
    units = nb * dn_heads
    mixed, dn_n, ret_n = pl.pallas_call(
        kern,
        grid=(batch // nb, nchunk + 1),
        in_specs=[pl.BlockSpec(memory_space=pltpu.SMEM),
                  pl.BlockSpec((nb, chunk, p_qkv.shape[1]), lambda g, n: (g, cur(n), 0)),
                  pl.BlockSpec((nb, chunk, LANES), lambda g, n: (g, cur(n), 0)),
                  pl.BlockSpec((nb, chunk, p_rest.shape[1]), lambda g, n: (g, prev(n), 0)),
                  full(gate_par.shape), full(dn_norm.shape), full(inv_freq.shape),
                  state(dn0), state(ret0)],
        out_specs=[pl.BlockSpec((nb, chunk, mix_w), lambda g, n: (g, prev(n), 0)),
                   state(dn0), state(ret0)],
        out_shape=[jax.ShapeDtypeStruct((batch, t, mix_w), BF16),
                   jax.ShapeDtypeStruct(dn0.shape, F32),
                   jax.ShapeDtypeStruct(ret0.shape, F32)],
        scratch_shapes=[pltpu.VMEM((2, units, chunk, chunk), F32),
                        pltpu.VMEM((2, units, chunk, chunk), BF16),
                        pltpu.VMEM((2, units, chunk, DN_DV + DN_DK), F32),
                        pltpu.VMEM((2, units, chunk, DN_DK), BF16),
                        pltpu.VMEM((2, units, chunk, DN_DK), BF16),
                        pltpu.VMEM((2, units, 1, LANES), F32)],
        compiler_params=_params("parallel", "arbitrary"),
        name="mixer",
    )(log_gamma, p_qkv.reshape(batch, t, -1), p_gate.reshape(batch, t, LANES),
      p_rest.reshape(batch, t, -1), gate_par, dn_norm, inv_freq, dn0, ret0)
    return mixed.reshape(m, mix_w), dn_n, ret_n


def _ffn_kernel(x_ref, mix_ref, gm_ref, wo_ref, nw_ref, sh_ref, sc_ref, gf_ref, wg_ref, wu_ref, wd_ref,
                nf_ref, shf_ref, scf_ref, y_ref, x1_ref, h_ref):
    j = pl.program_id(1)

    @pl.when(j == 0)
    def _():
        x1 = x_ref[...] + gm_ref[0] * jnp.dot(mix_ref[...], wo_ref[...], preferred_element_type=F32)
        x1_ref[...] = x1
        h = _norm_mod(x1, nw_ref[...], sh_ref[0], sc_ref[0])
        h_ref[...] = h.astype(BF16)
        y_ref[...] = jnp.zeros_like(y_ref)

    hb = h_ref[...]
    gate = jnp.dot(hb, wg_ref[...], preferred_element_type=F32)
    up = jnp.dot(hb, wu_ref[...], preferred_element_type=F32)
    act = (_silu(gate) * up).astype(BF16)
    y_ref[...] += jnp.dot(act, wd_ref[...], preferred_element_type=F32)

    @pl.when(j == pl.num_programs(1) - 1)
    def _():
        x2 = x1_ref[...] + gf_ref[0] * y_ref[...]
        y_ref[...] = _norm_mod(x2, nf_ref[...], shf_ref[0], scf_ref[0])


def _ffn(x, mixed, w_out, nw, ada, mod_map, w_gu, w_down, nf, ada_fin, fin_map, tm, tf):
    m, d = x.shape
    kmix = mixed.shape[1]
    dff = w_down.shape[0]
    nf_blocks = dff // tf
    r = ada.shape[1]
    mod = lambda k: pl.BlockSpec((1, r, d), lambda i, j: (mod_map(i, k), 0, 0))
    fin = lambda k: pl.BlockSpec((1, r, d), lambda i, j: (fin_map(i, k), 0, 0))
    return pl.pallas_call(
        _ffn_kernel,
        grid=(m // tm, nf_blocks),
        in_specs=[pl.BlockSpec((tm, d), lambda i, j: (i, 0)),
                  pl.BlockSpec((tm, kmix), lambda i, j: (i, 0)),
                  mod(2),
                  pl.BlockSpec((kmix, d), lambda i, j: (0, 0), pipeline_mode=pl.Buffered(1)),
                  pl.BlockSpec((1, d), lambda i, j: (0, 0)),
                  mod(3), mod(4), mod(5),
                  pl.BlockSpec((d, tf), lambda i, j: (0, j)),
                  pl.BlockSpec((d, tf), lambda i, j: (0, j + nf_blocks)),
                  pl.BlockSpec((tf, d), lambda i, j: (j, 0)),
                  pl.BlockSpec((1, d), lambda i, j: (0, 0)),
                  fin(0), fin(1)],
        out_specs=pl.BlockSpec((tm, d), lambda i, j: (i, 0)),
        out_shape=jax.ShapeDtypeStruct((m, d), F32),
        scratch_shapes=[pltpu.VMEM((tm, d), F32), pltpu.VMEM((tm, d), BF16)],
        compiler_params=_params("parallel", "arbitrary"),
        name="ffn",
    )(x, mixed, ada, w_out, nw, ada, ada, ada, w_gu, w_gu, w_down, nf, ada_fin, ada_fin)


def _to_blocks(a, heads):
    shp = a.shape
    kinds = shp[-1] // (heads * LANES)
    a = a.reshape(shp[:-1] + (kinds, heads, LANES))
    return jnp.swapaxes(a, -3, -2).reshape(shp)


def _from_blocks(a, heads):
    shp = a.shape
    kinds = shp[-1] // (heads * LANES)
    a = a.reshape(shp[:-1] + (heads, kinds, LANES))
    return jnp.swapaxes(a, -3, -2).reshape(shp)


class _Tiles(NamedTuple):
    proj_rows: int
    ffn_rows: int
    ffn_cols: int


def _stream(x, ada, ada_fin, conv0, dn0, ret0, weights, *, chunk, pos0, tiles, per_row):
    (nw_mix, nw_ffn, w_main, w_gate, conv_w, gate_par, dn_norm, inv_freq, log_gamma,
     w_out, w_gu, w_down, nf) = weights
    b, t, d = x.shape
    m = b * t
    xf = x.reshape(m, d)
    if per_row:
        ada_r = jnp.repeat(ada.reshape(b, N_ADA, d), t, axis=0).transpose(1, 0, 2)
        fin_r = jnp.repeat(ada_fin.reshape(b, 2, d), t, axis=0).transpose(1, 0, 2)
        mod_map = lambda tm: (lambda i, k: k)
        fin_map = lambda tm: (lambda i, k: k)
    else:
        assert t % tiles.proj_rows == 0 and t % tiles.ffn_rows == 0
        ada_r = ada.reshape(b * N_ADA, 1, d)
        fin_r = ada_fin.reshape(b * 2, 1, d)
        mod_map = lambda tm: (lambda i, k: (i // (t // tm)) * N_ADA + k)
        fin_map = lambda tm: (lambda i, k: (i // (t // tm)) * 2 + k)

    p_qkv, p_rest, p_gate, conv_n = _in_proj(xf, nw_mix, ada_r, ada_r, mod_map(tiles.proj_rows),
                                             w_main, w_gate, conv_w, conv0, t, tiles.proj_rows)
    mixed, dn_n, ret_n = _mixer(p_qkv, p_rest, p_gate, gate_par, dn_norm, inv_freq, log_gamma,
                                dn0, ret0, batch=b, chunk=chunk, pos0=pos0)
    y = _ffn(xf, mixed, w_out, nw_ffn, ada_r, mod_map(tiles.ffn_rows), w_gu, w_down, nf, fin_r,
             fin_map(tiles.ffn_rows), tiles.ffn_rows, tiles.ffn_cols)
    return y.reshape(b, t, d), conv_n, dn_n, ret_n


def kernel(x_prompt, x_sample, state_conv, state_delta, state_ret, c_prompt, c_sample, norm_mix, norm_ffn, w_ada, b_ada, w_in, conv_w, dn_a_log, dn_dt_bias, dn_norm, w_out, w_gu, w_down, norm_final, w_ada_final, b_ada_final):
    bp, tp, d = x_prompt.shape
    bs, ts, _ = x_sample.shape
    depth = w_in.shape[0]
    assert depth == 1, "single-layer pipeline"
    dn_heads = state_delta.shape[2]
    ret_heads = state_ret.shape[2]
    conv_ch = conv_w.shape[2]
    assert 2 * dn_heads <= LANES

    nb = bp + bs
    pad = (-nb) % (2 * SUBLANES)
    c_all = jnp.concatenate([c_prompt, c_sample, jnp.zeros((pad, d), F32)], axis=0)
    ada_all = _ada(c_all, w_ada[0], b_ada[0], 1024)
    fin_all = _ada(c_all, w_ada_final, b_ada_final, 1024)

    l = 0
    g_off = conv_ch + dn_heads * DN_DV
    r_off = g_off + 2 * dn_heads
    w_l = w_in[l]
    w_kinds = jnp.concatenate([w_l[:, :g_off], w_l[:, r_off:]], axis=1)
    w_main = _to_blocks(w_kinds, dn_heads).astype(BF16)
    w_gate = jnp.concatenate([w_l[:, g_off:r_off],
                              jnp.zeros((d, LANES - 2 * dn_heads), F32)], axis=1).astype(BF16)
    lane_pad = jnp.zeros((LANES - 2 * dn_heads,), F32)
    gate_par = jnp.stack([
        jnp.concatenate([jnp.zeros((dn_heads,), F32), -jnp.exp(dn_a_log[l].astype(F32)), lane_pad]),
        jnp.concatenate([jnp.zeros((dn_heads,), F32), dn_dt_bias[l].astype(F32), lane_pad])])
    inv_freq = 1.0 / (ROPE_BASE ** jnp.linspace(0.0, 1.0, RET_DK // 2, dtype=F32))
    inv_freq = jnp.stack([-inv_freq, inv_freq], axis=-1).reshape(1, RET_DK)
    log_gamma = jnp.log(1.0 - 2.0 ** (-5.0 - jnp.arange(ret_heads, dtype=F32)))
    weights = (norm_mix[l].reshape(1, d), norm_ffn[l].reshape(1, d), w_main, w_gate,
               _to_blocks(conv_w[l], dn_heads),
               gate_par, dn_norm[l].reshape(1, -1), inv_freq, log_gamma,
               w_out[l].astype(BF16), w_gu[l].astype(BF16), w_down[l].astype(BF16),
               norm_final.reshape(1, d))

    zc = jnp.zeros((bp, CONV_W - 1, conv_ch), F32)
    zd = jnp.zeros((bp,) + state_delta.shape[2:], F32)
    zr = jnp.zeros((bp,) + state_ret.shape[2:], F32)

    y_p, cp, dp, rp = _stream(x_prompt, ada_all[:bp], fin_all[:bp], zc, zd, zr, weights,
                              chunk=CHUNK, pos0=0, per_row=False,
                              tiles=_Tiles(proj_rows=1024, ffn_rows=512, ffn_cols=512))
    y_s, cs, ds, rs = _stream(x_sample, ada_all[bp:nb], fin_all[bp:nb],
                              _to_blocks(state_conv[l], dn_heads), state_delta[l], state_ret[l], weights,
                              chunk=ts, pos0=PAST_LEN, per_row=True,
                              tiles=_Tiles(proj_rows=bs * ts, ffn_rows=bs * ts, ffn_cols=512))
    cp, cs = _from_blocks(cp, dn_heads), _from_blocks(cs, dn_heads)
    return (y_p, y_s, cp[None], dp[None], rp[None], cs[None], ds[None], rs[None])
```

```python
import functools
from typing import NamedTuple

import jax
import jax.numpy as jnp
from jax import lax
from jax.experimental import pallas as pl
from jax.experimental.pallas import tpu as pltpu

F32 = jnp.float32
BF16 = jnp.bfloat16

DN_DK = 128
DN_DV = 128
RET_DK = 256
RET_DV = 256
CONV_W = 4
N_ADA = 6
CHUNK = 64
PAST_LEN = 2048
ROPE_BASE = 10000.0
EPS = 1e-6

LANES = 128
SUBLANES = 8
VMEM_LIMIT = 56 * 1024 * 1024


def _silu(x):
    return x * jax.nn.sigmoid(x)


def _dot(a, b):
    return jnp.dot(a.astype(BF16), b.astype(BF16), preferred_element_type=F32)


def _dot_nt(a, b):
    return lax.dot_general(a.astype(BF16), b.astype(BF16), (((1,), (1,)), ((), ())),
                           preferred_element_type=F32)


def _dot_tn(a, b):
    return lax.dot_general(a.astype(BF16), b.astype(BF16), (((0,), (0,)), ((), ())),
                           preferred_element_type=F32)


def _params(*sem):
    return pltpu.CompilerParams(dimension_semantics=sem, vmem_limit_bytes=VMEM_LIMIT)


def _ada_kernel(c_ref, w_ref, b_ref, o_ref):
    s = _silu(c_ref[...])
    o_ref[...] = _dot(s, w_ref[...]) + b_ref[...]


def _ada(c_all, w, b, tn):
    m, d = c_all.shape
    n = w.shape[1]
    return pl.pallas_call(
        _ada_kernel,
        grid=(n // tn,),
        in_specs=[pl.BlockSpec((m, d), lambda j: (0, 0)),
                  pl.BlockSpec((d, tn), lambda j: (0, j)),
                  pl.BlockSpec((1, tn), lambda j: (0, j))],
        out_specs=pl.BlockSpec((m, tn), lambda j: (0, j)),
        out_shape=jax.ShapeDtypeStruct((m, n), F32),
        compiler_params=_params("arbitrary"),
        name="ada",
    )(c_all, w, b.reshape(1, n))


def _norm_mod(x, nw, shift, scale):
    y = x * lax.rsqrt(jnp.mean(x * x, axis=-1, keepdims=True) + EPS)
    y = y * nw
    return y * (1.0 + scale) + shift


HIST = SUBLANES
QKV_W = 2 * DN_DK + DN_DV
REST_PIECES = 5
REST_W = REST_PIECES * LANES


def _proj_kernel(x_ref, nw_ref, sh_ref, sc_ref, w_ref, wg_ref, cw_ref, conv0_ref,
                 pqkv_ref, prest_ref, pg_ref, convn_ref, h_ref, pe_ref, tail_ref,
                 *, seg, tiles_per_batch):
    i = pl.program_id(0)
    j = pl.program_id(1)
    tm = pqkv_ref.shape[0]
    nseg = tm // seg

    @pl.when(j == 0)
    def _():
        h = _norm_mod(x_ref[...], nw_ref[...], sh_ref[0], sc_ref[0])
        h_ref[...] = h.astype(BF16)
        pg_ref[...] = jnp.dot(h_ref[...], wg_ref[...], preferred_element_type=F32)

    acc = jnp.dot(h_ref[...], w_ref[...], preferred_element_type=F32)
    prest_ref[...] = acc[:, QKV_W:]

    for s in range(nseg):
        rows = acc[s * seg:(s + 1) * seg, 0:QKV_W]
        hist = jnp.concatenate([jnp.zeros((HIST - (CONV_W - 1), QKV_W), F32), conv0_ref[s]], axis=0)
        if tiles_per_batch > 1:
            hist = jnp.where(i % tiles_per_batch == 0, hist, tail_ref[j])
            tail_ref[j] = rows[seg - HIST:]
        pe_ref[s, 0:HIST, :] = hist
        pe_ref[s, HIST:HIST + seg, :] = rows
        convn_ref[s] = rows[seg - (CONV_W - 1):]
        out = pe_ref[s, pl.ds(HIST, seg), :] * cw_ref[CONV_W - 1:CONV_W, :]
        for t in range(1, CONV_W):
            out = out + pe_ref[s, pl.ds(HIST - t, seg), :] * cw_ref[CONV_W - 1 - t:CONV_W - t, :]
        out = _silu(out)
        q = out[:, 0:DN_DK]
        k = out[:, DN_DK:2 * DN_DK]
        q = q * lax.rsqrt(jnp.sum(q * q, axis=-1, keepdims=True) + EPS) * (DN_DK ** -0.5)
        k = k * lax.rsqrt(jnp.sum(k * k, axis=-1, keepdims=True) + EPS)
        pqkv_ref[s * seg:(s + 1) * seg, 0:DN_DK] = q
        pqkv_ref[s * seg:(s + 1) * seg, DN_DK:2 * DN_DK] = k
        pqkv_ref[s * seg:(s + 1) * seg, 2 * DN_DK:] = out[:, 2 * DN_DK:]


def _in_proj(x, nw, sh, sc, mod_map, w_main, w_gate, conv_w, conv0, t, tm):
    m, d = x.shape
    tn = QKV_W + REST_W
    nblk = w_main.shape[1] // tn
    r = sh.shape[1]
    assert conv_w.shape[1] == nblk * QKV_W
    seg = min(tm, t)
    assert tm % seg == 0 and t % seg == 0 and seg >= HIST
    nseg = tm // seg
    tiles_per_batch = t // seg
    assert nseg == 1 or tiles_per_batch == 1
    batch_blk = (lambda i: i // tiles_per_batch) if nseg == 1 else (lambda i: i)
    kern = functools.partial(_proj_kernel, seg=seg, tiles_per_batch=tiles_per_batch)
    p_qkv, p_rest, p_gate, conv_seg = pl.pallas_call(
        kern,
        grid=(m // tm, nblk),
        in_specs=[pl.BlockSpec((tm, d), lambda i, j: (i, 0)),
                  pl.BlockSpec((1, d), lambda i, j: (0, 0)),
                  pl.BlockSpec((1, r, d), lambda i, j: (mod_map(i, 0), 0, 0)),
                  pl.BlockSpec((1, r, d), lambda i, j: (mod_map(i, 1), 0, 0)),
                  pl.BlockSpec((d, tn), lambda i, j: (0, j)),
                  pl.BlockSpec((d, LANES), lambda i, j: (0, 0)),
                  pl.BlockSpec((CONV_W, QKV_W), lambda i, j: (0, j)),
                  pl.BlockSpec((nseg, CONV_W - 1, QKV_W), lambda i, j: (batch_blk(i), 0, j))],
        out_specs=[pl.BlockSpec((tm, QKV_W), lambda i, j: (i, j)),
                   pl.BlockSpec((tm, REST_W), lambda i, j: (i, j)),
                   pl.BlockSpec((tm, LANES), lambda i, j: (i, 0)),
                   pl.BlockSpec((nseg, CONV_W - 1, QKV_W), lambda i, j: (i, 0, j))],
        out_shape=[jax.ShapeDtypeStruct((m, nblk * QKV_W), F32),
                   jax.ShapeDtypeStruct((m, nblk * REST_W), F32),
                   jax.ShapeDtypeStruct((m, LANES), F32),
                   jax.ShapeDtypeStruct((m // seg,) + conv0.shape[1:], F32)],
        scratch_shapes=[pltpu.VMEM((tm, d), BF16),
                        pltpu.VMEM((nseg, HIST + seg, QKV_W), F32),
                        pltpu.VMEM((nblk, HIST, QKV_W), F32)],
        compiler_params=_params("arbitrary", "arbitrary"),
        name="in_proj",
    )(x, nw, sh, sc, w_main, w_gate, conv_w, conv0)
    conv_n = conv_seg.reshape((m // t, tiles_per_batch) + conv0.shape[1:])[:, -1]
    return p_qkv, p_rest, p_gate, conv_n


def _split(a):
    hi = a.astype(BF16)
    lo = (a - hi.astype(F32)).astype(BF16)
    return hi, lo


def _dot_split(a_parts, b_parts):
    ah, al = a_parts
    bh, bl = b_parts
    rows = ah.shape[0]
    top = jnp.dot(jnp.concatenate([ah, al], axis=0), bh, preferred_element_type=F32)
    return top[:rows] + top[rows:] + jnp.dot(ah, bl, preferred_element_type=F32)


def _neumann_correction(ms, fillers=()):
    c = ms[0].shape[0]
    fillers = list(fillers)

    def fill():
        if fillers:
            fillers.pop(0)()

    ns = [-m for m in ms]
    pws = [_dot_split(_split(m), _split(m)) for m in ms]
    fill()
    k = 2
    while k < c:
        pps = [_split(pw) for pw in pws]
        nps = [_split(n) for n in ns]
        if 2 * k < c:
            tops = [_dot_split((jnp.concatenate([pp[0], npart[0]], axis=0),
                                jnp.concatenate([pp[1], npart[1]], axis=0)), pp)
                    for pp, npart in zip(pps, nps)]
            fill()
            ns = [n + pw + top[c:] for n, pw, top in zip(ns, pws, tops)]
            pws = [top[:c] for top in tops]
        else:
            prods = [_dot_split(npart, pp) for npart, pp in zip(nps, pps)]
            fill()
            ns = [n + pw + prod for n, pw, prod in zip(ns, pws, prods)]
        k *= 2
    while fillers:
        fill()
    return ns


def _mixer_kernel(lg_ref, pqkv_ref, pg_ref, prest_ref, gp_ref, dnw_ref, inv_ref, dn0_ref, ret0_ref,
                  mix_ref, dn_ref, ret_ref,
                  m_s, qkg_s, rhs_s, qd_s, kd_s, cd_s,
                  *, chunk, nb, dn_heads, ret_heads, pos0):
    c = chunk
    n = pl.program_id(1)
    batches = range(nb)
    units = [(b, h) for b in batches for h in range(dn_heads)]
    runits = [(b, h) for b in batches for h in range(ret_heads)]
    uid = lambda u: u[0] * dn_heads + u[1]
    wr = n % 2
    rd = 1 - wr

    @pl.when(n == 0)
    def _():
        m_s[...] = jnp.zeros_like(m_s)
        qkg_s[...] = jnp.zeros_like(qkg_s)
        rhs_s[...] = jnp.zeros_like(rhs_s)
        qd_s[...] = jnp.zeros_like(qd_s)
        kd_s[...] = jnp.zeros_like(kd_s)
        cd_s[...] = jnp.zeros_like(cd_s)
        dn_ref[...] = dn0_ref[...]
        ret_ref[...] = ret0_ref[...]

    row = lax.broadcasted_iota(jnp.int32, (c, c), 0)
    col = lax.broadcasted_iota(jnp.int32, (c, c), 1)
    causal = row >= col
    strict = row > col

    mix_off = dn_heads * DN_DV

    beta_all, cs = [], []
    for b in batches:
        pg = pg_ref[b]
        beta_all.append(jax.nn.sigmoid(pg))
        g_all = gp_ref[0:1, :] * jax.nn.softplus(pg + gp_ref[1:2, :])
        g_hi, g_lo = _split(g_all)
        g_lo2 = (g_all - g_hi.astype(F32) - g_lo.astype(F32)).astype(BF16)
        cs.append(jnp.dot(causal.astype(BF16), jnp.concatenate([g_hi, g_lo, g_lo2], axis=-1),
                          preferred_element_type=F32))

    prep = {}

    def prep_values():
        for b in batches:
            decay_all = cs[b][:, :LANES] + cs[b][:, LANES:2 * LANES] + cs[b][:, 2 * LANES:]
            decay_t = decay_all.T
            for h in range(dn_heads):
                i = uid((b, h))
                q = pqkv_ref[b, :, h * QKV_W:h * QKV_W + DN_DK]
                k = pqkv_ref[b, :, h * QKV_W + DN_DK:h * QKV_W + 2 * DN_DK]
                v = pqkv_ref[b, :, h * QKV_W + 2 * DN_DK:(h + 1) * QKV_W]
                beta = beta_all[b][:, h:h + 1]
                dcol = decay_all[:, dn_heads + h:dn_heads + h + 1]
                drow = decay_t[dn_heads + h:dn_heads + h + 1, :]
                last = drow[:, c - 1:c]
                kb = k * beta
                edec = jnp.exp(dcol)
                prep[i] = dict(k=k, lhs=jnp.concatenate([kb, q], axis=0).astype(BF16),
                               gam=jnp.where(causal, jnp.exp(jnp.where(causal, dcol - drow, 0.0)), 0.0))
                rhs_s[wr, i] = jnp.concatenate([v * beta, kb * edec], axis=-1)
                qd_s[wr, i] = (q * edec).astype(BF16)
                kd_s[wr, i] = (k * jnp.exp(last - dcol)).astype(BF16)
                cd_s[wr, i] = jnp.broadcast_to(jnp.exp(last), (1, LANES))

    def prep_matmuls():
        for u in units:
            p = prep[uid(u)]
            p["both"] = _dot_nt(p["lhs"], p["k"])

    def prep_store():
        for u in units:
            i = uid(u)
            both, gam = prep[i]["both"], prep[i]["gam"]
            m_s[wr, i] = jnp.where(strict, both[:c] * gam, 0.0)
            qkg_s[wr, i] = (both[c:] * gam).astype(BF16)

    tok = lax.broadcasted_iota(jnp.int32, (c, 1), 0)
    idx = tok.astype(F32)
    ret = {}

    def piece(b, blk, kind):
        off = blk * REST_W + kind * LANES
        return prest_ref[b, :, off:off + LANES]

    even_lane = lax.broadcasted_iota(jnp.int32, (c, LANES), 1) % 2 == 0

    def rot(b, h, kind, cos, sin):
        halves = []
        for p in range(2):
            x = piece(b, 2 * h + p, kind)
            partner = jnp.where(even_lane, pltpu.roll(x, LANES - 1, axis=1), pltpu.roll(x, 1, axis=1))
            lanes = slice(p * LANES, (p + 1) * LANES)
            halves.append(x * cos[:, lanes] + partner * sin[:, lanes])
        return jnp.concatenate(halves, axis=-1)

    def wide(b, h, kind):
        return jnp.concatenate([piece(b, 2 * h, kind), piece(b, 2 * h + 1, kind)], axis=-1)

    def ret_values():
        pos = (pos0 + (n - 1) * c + tok).astype(F32)
        ang = pos * inv_ref[...]
        cos = jnp.cos(ang)
        sin = jnp.sin(ang)
        for u in runits:
            b, h = u
            lg = lg_ref[h]
            rq = rot(b, h, 1, cos, sin)
            rk = rot(b, h, 2, cos, sin) * (RET_DK ** -0.5)
            ret[u] = dict(rq=rq.astype(BF16), rk=rk.astype(BF16),
                          rv=wide(b, h, 3).astype(BF16),
                          qx=(rq * jnp.exp((idx + 1.0) * lg)).astype(BF16),
                          kz=(rk * jnp.exp((c - 1.0 - idx) * lg)).astype(BF16))

    def ret_scores():
        for u in runits:
            ret[u]["qk"] = _dot_nt(ret[u]["rq"], ret[u]["rk"])

    def ret_inner():
        dpos = (row - col).astype(F32)
        dmask = [jnp.where(causal, jnp.exp(jnp.where(causal, dpos, 0.0) * lg_ref[h]), 0.0)
                 for h in range(ret_heads)]
        for u in runits:
            ret[u]["inner"] = jnp.dot((ret[u]["qk"] * dmask[u[1]]).astype(BF16), ret[u]["rv"],
                                      preferred_element_type=F32)

    ns = _neumann_correction([m_s[rd, uid(u)] for u in units],
                             [prep_values, prep_matmuls, prep_store, ret_values, ret_scores, ret_inner])
    rhs = [rhs_s[rd, uid(u)] for u in units]
    sols = [r + _dot_split(_split(nn), _split(r)) for r, nn in zip(rhs, ns)]

    sd = [dn_ref[b, h] for b, h in units]
    ws = [jnp.dot(jnp.concatenate([sol[:, DN_DV:].astype(BF16), qd_s[rd, uid(u)]], axis=0),
                  s.astype(BF16), preferred_element_type=F32) for u, sol, s in zip(units, sols, sd)]
    sr = [ret_ref[b, h] for b, h in runits]
    ro = [ret[u]["inner"] + jnp.dot(ret[u]["qx"], s.astype(BF16), preferred_element_type=F32)
          for u, s in zip(runits, sr)]
    rupd = [_dot_tn(ret[u]["kz"], ret[u]["rv"]) for u in runits]
    us = [(sol[:, :DN_DV] - w[:c]).astype(BF16) for sol, w in zip(sols, ws)]
    o2 = [jnp.dot(qkg_s[rd, uid(u)], uu, preferred_element_type=F32) for u, uu in zip(units, us)]
    upd = [_dot_tn(kd_s[rd, uid(u)], uu) for u, uu in zip(units, us)]
    cdr = [jnp.exp(jnp.zeros((1, 1), F32) + c * lg_ref[h]) for h in range(ret_heads)]
    for k, (b, h) in enumerate(runits):
        ret_ref[b, h] = sr[k] * cdr[h] + rupd[k]
        o = ro[k]
        o = o * lax.rsqrt(jnp.mean(o * o, axis=-1, keepdims=True) + EPS)
        gg = wide(b, h, 4)
        mix_ref[b, :, mix_off + h * RET_DV:mix_off + (h + 1) * RET_DV] = (
            (o * _silu(gg)).astype(mix_ref.dtype))
    for k, (b, h) in enumerate(units):
        dn_ref[b, h] = sd[k] * cd_s[rd, uid((b, h))][:, 0:1] + upd[k]
        o = ws[k][c:] + o2[k]
        o = o * lax.rsqrt(jnp.mean(o * o, axis=-1, keepdims=True) + EPS) * dnw_ref[...]
        zz = piece(b, h, 0)
        mix_ref[b, :, h * DN_DV:(h + 1) * DN_DV] = (o * _silu(zz)).astype(mix_ref.dtype)

    @pl.when(n == 0)
    def _():
        dn_ref[...] = dn0_ref[...]
        ret_ref[...] = ret0_ref[...]


def _mixer(p_qkv, p_rest, p_gate, gate_par, dn_norm, inv_freq, log_gamma, dn0, ret0,
           *, batch, chunk, pos0):
    m = p_qkv.shape[0]
    t = m // batch
    nchunk = t // chunk
    dn_heads = dn0.shape[1]
    ret_heads = ret0.shape[1]
    mix_w = dn_heads * DN_DV + ret_heads * RET_DV
    assert p_qkv.shape[1] == dn_heads * QKV_W and p_rest.shape[1] == dn_heads * REST_W
    assert dn_heads == 2 * ret_heads and RET_DK == 2 * LANES and RET_DV == 2 * LANES
    nb = 2 if batch % 2 == 0 else 1
    kern = functools.partial(_mixer_kernel, chunk=chunk, nb=nb, dn_heads=dn_heads,
                             ret_heads=ret_heads, pos0=pos0)
    full = lambda shape: pl.BlockSpec(shape, lambda g, n: (0,) * len(shape))
    cur = lambda n: jnp.minimum(n, nchunk - 1)
    prev = lambda n: jnp.maximum(n - 1, 0)
    state = lambda a: pl.BlockSpec((nb,) + a.shape[1:], lambda g, n: (g,) + (0,) * (a.ndim - 1))
    units = nb * dn_heads
    mixed, dn_n, ret_n = pl.pallas_call(
        kern,
        grid=(batch // nb, nchunk + 1),
        in_specs=[pl.BlockSpec(memory_space=pltpu.SMEM),
                  pl.BlockSpec((nb, chunk, p_qkv.shape[1]), lambda g, n: (g, cur(n), 0)),
                  pl.BlockSpec((nb, chunk, LANES), lambda g, n: (g, cur(n), 0)),
                  pl.BlockSpec((nb, chunk, p_rest.shape[1]), lambda g, n: (g, prev(n), 0)),
                  full(gate_par.shape), full(dn_norm.shape), full(inv_freq.shape),
                  state(dn0), state(ret0)],
        out_specs=[pl.BlockSpec((nb, chunk, mix_w), lambda g, n: (g, prev(n), 0)),
                   state(dn0), state(ret0)],
        out_shape=[jax.ShapeDtypeStruct((batch, t, mix_w), BF16),
                   jax.ShapeDtypeStruct(dn0.shape, F32),
                   jax.ShapeDtypeStruct(ret0.shape, F32)],
        scratch_shapes=[pltpu.VMEM((2, units, chunk, chunk), F32),
                        pltpu.VMEM((2, units, chunk, chunk), BF16),
                        pltpu.VMEM((2, units, chunk, DN_DV + DN_DK), F32),
                        pltpu.VMEM((2, units, chunk, DN_DK), BF16),
                        pltpu.VMEM((2, units, chunk, DN_DK), BF16),
                        pltpu.VMEM((2, units, 1, LANES), F32)],
        compiler_params=_params("parallel", "arbitrary"),
        name="mixer",
    )(log_gamma, p_qkv.reshape(batch, t, -1), p_gate.reshape(batch, t, LANES),
      p_rest.reshape(batch, t, -1), gate_par, dn_norm, inv_freq, dn0, ret0)
    return mixed.reshape(m, mix_w), dn_n, ret_n


def _ffn_kernel(x_ref, mix_ref, gm_ref, wo_ref, nw_ref, sh_ref, sc_ref, gf_ref, wg_ref, wu_ref, wd_ref,
                nf_ref, shf_ref, scf_ref, y_ref, x1_ref, h_ref):
    j = pl.program_id(1)

    @pl.when(j == 0)
    def _():
        x1 = x_ref[...] + gm_ref[0] * jnp.dot(mix_ref[...], wo_ref[...], preferred_element_type=F32)
        x1_ref[...] = x1
        h = _norm_mod(x1, nw_ref[...], sh_ref[0], sc_ref[0])
        h_ref[...] = h.astype(BF16)
        y_ref[...] = jnp.zeros_like(y_ref)

    hb = h_ref[...]
    gate = jnp.dot(hb, wg_ref[...], preferred_element_type=F32)
    up = jnp.dot(hb, wu_ref[...], preferred_element_type=F32)
    act = (_silu(gate) * up).astype(BF16)
    y_ref[...] += jnp.dot(act, wd_ref[...], preferred_element_type=F32)

    @pl.when(j == pl.num_programs(1) - 1)
    def _():
        x2 = x1_ref[...] + gf_ref[0] * y_ref[...]
        y_ref[...] = _norm_mod(x2, nf_ref[...], shf_ref[0], scf_ref[0])


def _ffn(x, mixed, w_out, nw, ada, mod_map, w_gu, w_down, nf, ada_fin, fin_map, tm, tf):
    m, d = x.shape
    kmix = mixed.shape[1]
    dff = w_down.shape[0]
    nf_blocks = dff // tf
    r = ada.shape[1]
    mod = lambda k: pl.BlockSpec((1, r, d), lambda i, j: (mod_map(i, k), 0, 0))
    fin = lambda k: pl.BlockSpec((1, r, d), lambda i, j: (fin_map(i, k), 0, 0))
    return pl.pallas_call(
        _ffn_kernel,
        grid=(m // tm, nf_blocks),
        in_specs=[pl.BlockSpec((tm, d), lambda i, j: (i, 0)),
                  pl.BlockSpec((tm, kmix), lambda i, j: (i, 0)),
                  mod(2),
                  pl.BlockSpec((kmix, d), lambda i, j: (0, 0), pipeline_mode=pl.Buffered(1)),
                  pl.BlockSpec((1, d), lambda i, j: (0, 0)),
                  mod(3), mod(4), mod(5),
                  pl.BlockSpec((d, tf), lambda i, j: (0, j)),
                  pl.BlockSpec((d, tf), lambda i, j: (0, j + nf_blocks)),
                  pl.BlockSpec((tf, d), lambda i, j: (j, 0)),
                  pl.BlockSpec((1, d), lambda i, j: (0, 0)),
                  fin(0), fin(1)],
        out_specs=pl.BlockSpec((tm, d), lambda i, j: (i, 0)),
        out_shape=jax.ShapeDtypeStruct((m, d), F32),
        scratch_shapes=[pltpu.VMEM((tm, d), F32), pltpu.VMEM((tm, d), BF16)],
        compiler_params=_params("parallel", "arbitrary"),
        name="ffn",
    )(x, mixed, ada, w_out, nw, ada, ada, ada, w_gu, w_gu, w_down, nf, ada_fin, ada_fin)


def _to_blocks(a, heads):
    shp = a.shape
    kinds = shp[-1] // (heads * LANES)
    a = a.reshape(shp[:-1] + (kinds, heads, LANES))
    return jnp.swapaxes(a, -3, -2).reshape(shp)


def _from_blocks(a, heads):
    shp = a.shape
    kinds = shp[-1] // (heads * LANES)
    a = a.reshape(shp[:-1] + (heads, kinds, LANES))
    return jnp.swapaxes(a, -3, -2).reshape(shp)


class _Tiles(NamedTuple):
    proj_rows: int
    ffn_rows: int
    ffn_cols: int


def _stream(x, ada, ada_fin, conv0, dn0, ret0, weights, *, chunk, pos0, tiles, per_row):
    (nw_mix, nw_ffn, w_main, w_gate, conv_w, gate_par, dn_norm, inv_freq, log_gamma,
     w_out, w_gu, w_down, nf) = weights
    b, t, d = x.shape
    m = b * t
    xf = x.reshape(m, d)
    if per_row:
        ada_r = jnp.repeat(ada.reshape(b, N_ADA, d), t, axis=0).transpose(1, 0, 2)
        fin_r = jnp.repeat(ada_fin.reshape(b, 2, d), t, axis=0).transpose(1, 0, 2)
        mod_map = lambda tm: (lambda i, k: k)
        fin_map = lambda tm: (lambda i, k: k)
    else:
        assert t % tiles.proj_rows == 0 and t % tiles.ffn_rows == 0
        ada_r = ada.reshape(b * N_ADA, 1, d)
        fin_r = ada_fin.reshape(b * 2, 1, d)
        mod_map = lambda tm: (lambda i, k: (i // (t // tm)) * N_ADA + k)
        fin_map = lambda tm: (lambda i, k: (i // (t // tm)) * 2 + k)

    p_qkv, p_rest, p_gate, conv_n = _in_proj(xf, nw_mix, ada_r, ada_r, mod_map(tiles.proj_rows),
                                             w_main, w_gate, conv_w, conv0, t, tiles.proj_rows)
    mixed, dn_n, ret_n = _mixer(p_qkv, p_rest, p_gate, gate_par, dn_norm, inv_freq, log_gamma,
                                dn0, ret0, batch=b, chunk=chunk, pos0=pos0)
    y = _ffn(xf, mixed, w_out, nw_ffn, ada_r, mod_map(tiles.ffn_rows), w_gu, w_down, nf, fin_r,
             fin_map(tiles.ffn_rows), tiles.ffn_rows, tiles.ffn_cols)
    return y.reshape(b, t, d), conv_n, dn_n, ret_n


def kernel(x_prompt, x_sample, state_conv, state_delta, state_ret, c_prompt, c_sample, norm_mix, norm_ffn, w_ada, b_ada, w_in, conv_w, dn_a_log, dn_dt_bias, dn_norm, w_out, w_gu, w_down, norm_final, w_ada_final, b_ada_final):
    bp, tp, d = x_prompt.shape
    bs, ts, _ = x_sample.shape
    depth = w_in.shape[0]
    assert depth == 1, "single-layer pipeline"
    dn_heads = state_delta.shape[2]
    ret_heads = state_ret.shape[2]
    conv_ch = conv_w.shape[2]
    assert 2 * dn_heads <= LANES

    nb = bp + bs
    pad = (-nb) % (2 * SUBLANES)
    c_all = jnp.concatenate([c_prompt, c_sample, jnp.zeros((pad, d), F32)], axis=0)
    ada_all = _ada(c_all, w_ada[0], b_ada[0], 1024)
    fin_all = _ada(c_all, w_ada_final, b_ada_final, 1024)

    l = 0
    g_off = conv_ch + dn_heads * DN_DV
    r_off = g_off + 2 * dn_heads
    w_l = w_in[l]
    by_head = lambda w: jnp.swapaxes(w.reshape(d, -1, dn_heads, LANES), 1, 2)
    w_main = jnp.concatenate([by_head(w_l[:, :g_off]), by_head(w_l[:, r_off:])],
                             axis=2).reshape(d, -1).astype(BF16)
    w_gate = jnp.concatenate([w_l[:, g_off:r_off],
                              jnp.zeros((d, LANES - 2 * dn_heads), F32)], axis=1).astype(BF16)
    lane_pad = jnp.zeros((LANES - 2 * dn_heads,), F32)
    gate_par = jnp.stack([
        jnp.concatenate([jnp.zeros((dn_heads,), F32), -jnp.exp(dn_a_log[l].astype(F32)), lane_pad]),
        jnp.concatenate([jnp.zeros((dn_heads,), F32), dn_dt_bias[l].astype(F32), lane_pad])])
    inv_freq = 1.0 / (ROPE_BASE ** jnp.linspace(0.0, 1.0, RET_DK // 2, dtype=F32))
    inv_freq = jnp.stack([-inv_freq, inv_freq], axis=-1).reshape(1, RET_DK)
    log_gamma = jnp.log(1.0 - 2.0 ** (-5.0 - jnp.arange(ret_heads, dtype=F32)))
    weights = (norm_mix[l].reshape(1, d), norm_ffn[l].reshape(1, d), w_main, w_gate,
               _to_blocks(conv_w[l], dn_heads),
               gate_par, dn_norm[l].reshape(1, -1), inv_freq, log_gamma,
               w_out[l].astype(BF16), w_gu[l].astype(BF16), w_down[l].astype(BF16),
               norm_final.reshape(1, d))

    zc = jnp.zeros((bp, CONV_W - 1, conv_ch), F32)
    zd = jnp.zeros((bp,) + state_delta.shape[2:], F32)
    zr = jnp.zeros((bp,) + state_ret.shape[2:], F32)

    y_p, cp, dp, rp = _stream(x_prompt, ada_all[:bp], fin_all[:bp], zc, zd, zr, weights,
                              chunk=CHUNK, pos0=0, per_row=False,
                              tiles=_Tiles(proj_rows=1024, ffn_rows=512, ffn_cols=512))
    y_s, cs, ds, rs = _stream(x_sample, ada_all[bp:nb], fin_all[bp:nb],
                              _to_blocks(state_conv[l], dn_heads), state_delta[l], state_ret[l], weights,
                              chunk=ts, pos0=PAST_LEN, per_row=True,
                              tiles=_Tiles(proj_rows=bs * ts, ffn_rows=bs * ts, ffn_cols=512))
    cp, cs = _from_blocks(cp, dn_heads), _from_blocks(cs, dn_heads)
    return (y_p, y_s, cp[None], dp[None], rp[None], cs[None], ds[None], rs[None])
```

```python
import functools
from typing import NamedTuple

import jax
import jax.numpy as jnp
from jax import lax
from jax.experimental import pallas as pl
from jax.experimental.pallas import tpu as pltpu

F32 = jnp.float32
BF16 = jnp.bfloat16

DN_DK = 128
DN_DV = 128
RET_DK = 256
RET_DV = 256
CONV_W = 4
N_ADA = 6
CHUNK = 64
PAST_LEN = 2048
ROPE_BASE = 10000.0
EPS = 1e-6

LANES = 128
SUBLANES = 8
VMEM_LIMIT = 56 * 1024 * 1024


def _silu(x):
    return x * jax.nn.sigmoid(x)


def _dot(a, b):
    return jnp.dot(a.astype(BF16), b.astype(BF16), preferred_element_type=F32)


def _dot_nt(a, b):
    return lax.dot_general(a.astype(BF16), b.astype(BF16), (((1,), (1,)), ((), ())),
                           preferred_element_type=F32)


def _dot_tn(a, b):
    return lax.dot_general(a.astype(BF16), b.astype(BF16), (((0,), (0,)), ((), ())),
                           preferred_element_type=F32)


def _params(*sem):
    return pltpu.CompilerParams(dimension_semantics=sem, vmem_limit_bytes=VMEM_LIMIT)


def _ada_kernel(c_ref, w_ref, b_ref, o_ref):
    s = _silu(c_ref[...])
    o_ref[...] = _dot(s, w_ref[...]) + b_ref[...]


def _ada(c_all, w, b, tn):
    m, d = c_all.shape
    n = w.shape[1]
    return pl.pallas_call(
        _ada_kernel,
        grid=(n // tn,),
        in_specs=[pl.BlockSpec((m, d), lambda j: (0, 0)),
                  pl.BlockSpec((d, tn), lambda j: (0, j)),
                  pl.BlockSpec((1, tn), lambda j: (0, j))],
        out_specs=pl.BlockSpec((m, tn), lambda j: (0, j)),
        out_shape=jax.ShapeDtypeStruct((m, n), F32),
        compiler_params=_params("arbitrary"),
        name="ada",
    )(c_all, w, b.reshape(1, n))


def _norm_mod(x, nw, shift, scale):
    y = x * lax.rsqrt(jnp.mean(x * x, axis=-1, keepdims=True) + EPS)
    y = y * nw
    return y * (1.0 + scale) + shift


HIST = SUBLANES
QKV_W = 2 * DN_DK + DN_DV
REST_PIECES = 5
REST_W = REST_PIECES * LANES


def _proj_kernel(x_ref, nw_ref, sh_ref, sc_ref, w_ref, wg_ref, cw_ref, conv0_ref,
                 pqkv_ref, prest_ref, pg_ref, convn_ref, h_ref, pe_ref, tail_ref,
                 *, seg, tiles_per_batch):
    i = pl.program_id(0)
    j = pl.program_id(1)
    tm = pqkv_ref.shape[0]
    nseg = tm // seg

    @pl.when(j == 0)
    def _():
        h = _norm_mod(x_ref[...], nw_ref[...], sh_ref[0], sc_ref[0])
        h_ref[...] = h.astype(BF16)
        pg_ref[...] = jnp.dot(h_ref[...], wg_ref[...], preferred_element_type=F32)

    acc = jnp.dot(h_ref[...], w_ref[...], preferred_element_type=F32)
    prest_ref[...] = acc[:, QKV_W:]

    for s in range(nseg):
        rows = acc[s * seg:(s + 1) * seg, 0:QKV_W]
        hist = jnp.concatenate([jnp.zeros((HIST - (CONV_W - 1), QKV_W), F32), conv0_ref[s]], axis=0)
        if tiles_per_batch > 1:
            hist = jnp.where(i % tiles_per_batch == 0, hist, tail_ref[j])
            tail_ref[j] = rows[seg - HIST:]
        pe_ref[s, 0:HIST, :] = hist
        pe_ref[s, HIST:HIST + seg, :] = rows
        convn_ref[s] = rows[seg - (CONV_W - 1):]
        out = pe_ref[s, pl.ds(HIST, seg), :] * cw_ref[CONV_W - 1:CONV_W, :]
        for t in range(1, CONV_W):
            out = out + pe_ref[s, pl.ds(HIST - t, seg), :] * cw_ref[CONV_W - 1 - t:CONV_W - t, :]
        out = _silu(out)
        q = out[:, 0:DN_DK]
        k = out[:, DN_DK:2 * DN_DK]
        q = q * lax.rsqrt(jnp.sum(q * q, axis=-1, keepdims=True) + EPS) * (DN_DK ** -0.5)
        k = k * lax.rsqrt(jnp.sum(k * k, axis=-1, keepdims=True) + EPS)
        pqkv_ref[s * seg:(s + 1) * seg, 0:DN_DK] = q
        pqkv_ref[s * seg:(s + 1) * seg, DN_DK:2 * DN_DK] = k
        pqkv_ref[s * seg:(s + 1) * seg, 2 * DN_DK:] = out[:, 2 * DN_DK:]


def _in_proj(x, nw, sh, sc, mod_map, w_main, w_gate, conv_w, conv0, t, tm):
    m, d = x.shape
    tn = QKV_W + REST_W
    nblk = w_main.shape[1] // tn
    r = sh.shape[1]
    assert conv_w.shape[1] == nblk * QKV_W
    seg = min(tm, t)
    assert tm % seg == 0 and t % seg == 0 and seg >= HIST
    nseg = tm // seg
    tiles_per_batch = t // seg
    assert nseg == 1 or tiles_per_batch == 1
    batch_blk = (lambda i: i // tiles_per_batch) if nseg == 1 else (lambda i: i)
    kern = functools.partial(_proj_kernel, seg=seg, tiles_per_batch=tiles_per_batch)
    p_qkv, p_rest, p_gate, conv_seg = pl.pallas_call(
        kern,
        grid=(m // tm, nblk),
        in_specs=[pl.BlockSpec((tm, d), lambda i, j: (i, 0)),
                  pl.BlockSpec((1, d), lambda i, j: (0, 0)),
                  pl.BlockSpec((1, r, d), lambda i, j: (mod_map(i, 0), 0, 0)),
                  pl.BlockSpec((1, r, d), lambda i, j: (mod_map(i, 1), 0, 0)),
                  pl.BlockSpec((d, tn), lambda i, j: (0, j)),
                  pl.BlockSpec((d, LANES), lambda i, j: (0, 0)),
                  pl.BlockSpec((CONV_W, QKV_W), lambda i, j: (0, j)),
                  pl.BlockSpec((nseg, CONV_W - 1, QKV_W), lambda i, j: (batch_blk(i), 0, j))],
        out_specs=[pl.BlockSpec((tm, QKV_W), lambda i, j: (i, j)),
                   pl.BlockSpec((tm, REST_W), lambda i, j: (i, j)),
                   pl.BlockSpec((tm, LANES), lambda i, j: (i, 0)),
                   pl.BlockSpec((nseg, CONV_W - 1, QKV_W), lambda i, j: (i, 0, j))],
        out_shape=[jax.ShapeDtypeStruct((m, nblk * QKV_W), F32),
                   jax.ShapeDtypeStruct((m, nblk * REST_W), F32),
                   jax.ShapeDtypeStruct((m, LANES), F32),
                   jax.ShapeDtypeStruct((m // seg,) + conv0.shape[1:], F32)],
        scratch_shapes=[pltpu.VMEM((tm, d), BF16),
                        pltpu.VMEM((nseg, HIST + seg, QKV_W), F32),
                        pltpu.VMEM((nblk, HIST, QKV_W), F32)],
        compiler_params=_params("arbitrary", "arbitrary"),
        name="in_proj",
    )(x, nw, sh, sc, w_main, w_gate, conv_w, conv0)
    conv_n = conv_seg.reshape((m // t, tiles_per_batch) + conv0.shape[1:])[:, -1]
    return p_qkv, p_rest, p_gate, conv_n


def _split(a):
    hi = a.astype(BF16)
    lo = (a - hi.astype(F32)).astype(BF16)
    return hi, lo


def _dot_split(a_parts, b_parts):
    ah, al = a_parts
    bh, bl = b_parts
    rows = ah.shape[0]
    top = jnp.dot(jnp.concatenate([ah, al], axis=0), bh, preferred_element_type=F32)
    return top[:rows] + top[rows:] + jnp.dot(ah, bl, preferred_element_type=F32)


def _neumann_correction(ms, fillers=()):
    c = ms[0].shape[0]
    fillers = list(fillers)

    def fill():
        if fillers:
            fillers.pop(0)()

    ns = [-m for m in ms]
    pws = [_dot_split(_split(m), _split(m)) for m in ms]
    fill()
    k = 2
    while k < c:
        pps = [_split(pw) for pw in pws]
        nps = [_split(n) for n in ns]
        if 2 * k < c:
            tops = [_dot_split((jnp.concatenate([pp[0], npart[0]], axis=0),
                                jnp.concatenate([pp[1], npart[1]], axis=0)), pp)
                    for pp, npart in zip(pps, nps)]
            fill()
            ns = [n + pw + top[c:] for n, pw, top in zip(ns, pws, tops)]
            pws = [top[:c] for top in tops]
        else:
            prods = [_dot_split(npart, pp) for npart, pp in zip(nps, pps)]
            fill()
            ns = [n + pw + prod for n, pw, prod in zip(ns, pws, prods)]
        k *= 2
    while fillers:
        fill()
    return ns


def _mixer_kernel(lg_ref, pqkv_ref, pg_ref, prest_ref, gp_ref, dnw_ref, inv_ref, dn0_ref, ret0_ref,
                  mix_ref, dn_ref, ret_ref,
                  m_s, qkg_s, rhs_s, qd_s, kd_s, cd_s,
                  *, chunk, nb, dn_heads, ret_heads, pos0):
    c = chunk
    n = pl.program_id(1)
    batches = range(nb)
    units = [(b, h) for b in batches for h in range(dn_heads)]
    runits = [(b, h) for b in batches for h in range(ret_heads)]
    uid = lambda u: u[0] * dn_heads + u[1]
    wr = n % 2
    rd = 1 - wr

    @pl.when(n == 0)
    def _():
        m_s[...] = jnp.zeros_like(m_s)
        qkg_s[...] = jnp.zeros_like(qkg_s)
        rhs_s[...] = jnp.zeros_like(rhs_s)
        qd_s[...] = jnp.zeros_like(qd_s)
        kd_s[...] = jnp.zeros_like(kd_s)
        cd_s[...] = jnp.zeros_like(cd_s)
        dn_ref[...] = dn0_ref[...]
        ret_ref[...] = ret0_ref[...]

    row = lax.broadcasted_iota(jnp.int32, (c, c), 0)
    col = lax.broadcasted_iota(jnp.int32, (c, c), 1)
    causal = row >= col
    strict = row > col

    mix_off = dn_heads * DN_DV

    beta_all, cs = [], []
    for b in batches:
        pg = pg_ref[b]
        beta_all.append(jax.nn.sigmoid(pg))
        g_all = gp_ref[0:1, :] * jax.nn.softplus(pg + gp_ref[1:2, :])
        g_hi, g_lo = _split(g_all)
        g_lo2 = (g_all - g_hi.astype(F32) - g_lo.astype(F32)).astype(BF16)
        cs.append(jnp.dot(causal.astype(BF16), jnp.concatenate([g_hi, g_lo, g_lo2], axis=-1),
                          preferred_element_type=F32))

    prep = {}

    def prep_values():
        for b in batches:
            decay_all = cs[b][:, :LANES] + cs[b][:, LANES:2 * LANES] + cs[b][:, 2 * LANES:]
            decay_t = decay_all.T
            for h in range(dn_heads):
                i = uid((b, h))
                q = pqkv_ref[b, :, h * QKV_W:h * QKV_W + DN_DK]
                k = pqkv_ref[b, :, h * QKV_W + DN_DK:h * QKV_W + 2 * DN_DK]
                v = pqkv_ref[b, :, h * QKV_W + 2 * DN_DK:(h + 1) * QKV_W]
                beta = beta_all[b][:, h:h + 1]
                dcol = decay_all[:, dn_heads + h:dn_heads + h + 1]
                drow = decay_t[dn_heads + h:dn_heads + h + 1, :]
                last = drow[:, c - 1:c]
                kb = k * beta
                edec = jnp.exp(dcol)
                prep[i] = dict(k=k, lhs=jnp.concatenate([kb, q], axis=0).astype(BF16),
                               gam=jnp.where(causal, jnp.exp(jnp.where(causal, dcol - drow, 0.0)), 0.0))
                rhs_s[wr, i] = jnp.concatenate([v * beta, kb * edec], axis=-1)
                qd_s[wr, i] = (q * edec).astype(BF16)
                kd_s[wr, i] = (k * jnp.exp(last - dcol)).astype(BF16)
                cd_s[wr, i] = jnp.broadcast_to(jnp.exp(last), (1, LANES))

    def prep_matmuls():
        for u in units:
            p = prep[uid(u)]
            p["both"] = _dot_nt(p["lhs"], p["k"])

    def prep_store():
        for u in units:
            i = uid(u)
            both, gam = prep[i]["both"], prep[i]["gam"]
            m_s[wr, i] = jnp.where(strict, both[:c] * gam, 0.0)
            qkg_s[wr, i] = (both[c:] * gam).astype(BF16)

    tok = lax.broadcasted_iota(jnp.int32, (c, 1), 0)
    idx = tok.astype(F32)
    ret = {}

    def piece(b, blk, kind):
        off = blk * REST_W + kind * LANES
        return prest_ref[b, :, off:off + LANES]

    even_lane = lax.broadcasted_iota(jnp.int32, (c, LANES), 1) % 2 == 0

    def rot(b, h, kind, cos, sin):
        halves = []
        for p in range(2):
            x = piece(b, 2 * h + p, kind)
            partner = jnp.where(even_lane, pltpu.roll(x, LANES - 1, axis=1), pltpu.roll(x, 1, axis=1))
            lanes = slice(p * LANES, (p + 1) * LANES)
            halves.append(x * cos[:, lanes] + partner * sin[:, lanes])
        return jnp.concatenate(halves, axis=-1)

    def wide(b, h, kind):
        return jnp.concatenate([piece(b, 2 * h, kind), piece(b, 2 * h + 1, kind)], axis=-1)

    def ret_values():
        pos = (pos0 + (n - 1) * c + tok).astype(F32)
        ang = pos * inv_ref[...]
        cos = jnp.cos(ang)
        sin = jnp.sin(ang)
        for u in runits:
            b, h = u
            lg = lg_ref[h]
            rq = rot(b, h, 1, cos, sin)
            rk = rot(b, h, 2, cos, sin) * (RET_DK ** -0.5)
            ret[u] = dict(rq=rq.astype(BF16), rk=rk.astype(BF16),
                          rv=wide(b, h, 3).astype(BF16),
                          qx=(rq * jnp.exp((idx + 1.0) * lg)).astype(BF16),
                          kz=(rk * jnp.exp((c - 1.0 - idx) * lg)).astype(BF16))

    def ret_scores():
        for u in runits:
            ret[u]["qk"] = _dot_nt(ret[u]["rq"], ret[u]["rk"])

    def ret_inner():
        dpos = (row - col).astype(F32)
        dmask = [jnp.where(causal, jnp.exp(jnp.where(causal, dpos, 0.0) * lg_ref[h]), 0.0)
                 for h in range(ret_heads)]
        for u in runits:
            ret[u]["inner"] = jnp.dot((ret[u]["qk"] * dmask[u[1]]).astype(BF16), ret[u]["rv"],
                                      preferred_element_type=F32)

    ns = _neumann_correction([m_s[rd, uid(u)] for u in units],
                             [prep_values, prep_matmuls, prep_store, ret_values, ret_scores, ret_inner])
    rhs = [rhs_s[rd, uid(u)] for u in units]
    sols = [r + _dot_split(_split(nn), _split(r)) for r, nn in zip(rhs, ns)]

    sd = [dn_ref[b, h] for b, h in units]
    ws = [jnp.dot(jnp.concatenate([sol[:, DN_DV:].astype(BF16), qd_s[rd, uid(u)]], axis=0),
                  s.astype(BF16), preferred_element_type=F32) for u, sol, s in zip(units, sols, sd)]
    sr = [ret_ref[b, h] for b, h in runits]
    ro = [ret[u]["inner"] + jnp.dot(ret[u]["qx"], s.astype(BF16), preferred_element_type=F32)
          for u, s in zip(runits, sr)]
    rupd = [_dot_tn(ret[u]["kz"], ret[u]["rv"]) for u in runits]
    us = [(sol[:, :DN_DV] - w[:c]).astype(BF16) for sol, w in zip(sols, ws)]
    o2 = [jnp.dot(qkg_s[rd, uid(u)], uu, preferred_element_type=F32) for u, uu in zip(units, us)]
    upd = [_dot_tn(kd_s[rd, uid(u)], uu) for u, uu in zip(units, us)]
    cdr = [jnp.exp(jnp.zeros((1, 1), F32) + c * lg_ref[h]) for h in range(ret_heads)]
    for k, (b, h) in enumerate(runits):
        ret_ref[b, h] = sr[k] * cdr[h] + rupd[k]
        o = ro[k]
        o = o * lax.rsqrt(jnp.mean(o * o, axis=-1, keepdims=True) + EPS)
        gg = wide(b, h, 4)
        mix_ref[b, :, mix_off + h * RET_DV:mix_off + (h + 1) * RET_DV] = (
            (o * _silu(gg)).astype(mix_ref.dtype))
    for k, (b, h) in enumerate(units):
        dn_ref[b, h] = sd[k] * cd_s[rd, uid((b, h))][:, 0:1] + upd[k]
        o = ws[k][c:] + o2[k]
        o = o * lax.rsqrt(jnp.mean(o * o, axis=-1, keepdims=True) + EPS) * dnw_ref[...]
        zz = piece(b, h, 0)
        mix_ref[b, :, h * DN_DV:(h + 1) * DN_DV] = (o * _silu(zz)).astype(mix_ref.dtype)

    @pl.when(n == 0)
    def _():
        dn_ref[...] = dn0_ref[...]
        ret_ref[...] = ret0_ref[...]


def _mixer(p_qkv, p_rest, p_gate, gate_par, dn_norm, inv_freq, log_gamma, dn0, ret0,
           *, batch, chunk, pos0):
    m = p_qkv.shape[0]
    t = m // batch
    nchunk = t // chunk
    dn_heads = dn0.shape[1]
    ret_heads = ret0.shape[1]
    mix_w = dn_heads * DN_DV + ret_heads * RET_DV
    assert p_qkv.shape[1] == dn_heads * QKV_W and p_rest.shape[1] == dn_heads * REST_W
    assert dn_heads == 2 * ret_heads and RET_DK == 2 * LANES and RET_DV == 2 * LANES
    nb = 2 if batch % 2 == 0 else 1
    kern = functools.partial(_mixer_kernel, chunk=chunk, nb=nb, dn_heads=dn_heads,
                             ret_heads=ret_heads, pos0=pos0)
    full = lambda shape: pl.BlockSpec(shape, lambda g, n: (0,) * len(shape))
    cur = lambda n: jnp.minimum(n, nchunk - 1)
    prev = lambda n: jnp.maximum(n - 1, 0)
    state = lambda a: pl.BlockSpec((nb,) + a.shape[1:], lambda g, n: (g,) + (0,) * (a.ndim - 1))
    units = nb * dn_heads
    mixed, dn_n, ret_n = pl.pallas_call(
        kern,
        grid=(batch // nb, nchunk + 1),
        in_specs=[pl.BlockSpec(memory_space=pltpu.SMEM),
                  pl.BlockSpec((nb, chunk, p_qkv.shape[1]), lambda g, n: (g, cur(n), 0)),
                  pl.BlockSpec((nb, chunk, LANES), lambda g, n: (g, cur(n), 0)),
                  pl.BlockSpec((nb, chunk, p_rest.shape[1]), lambda g, n: (g, prev(n), 0)),
                  full(gate_par.shape), full(dn_norm.shape), full(inv_freq.shape),
                  state(dn0), state(ret0)],
        out_specs=[pl.BlockSpec((nb, chunk, mix_w), lambda g, n: (g, prev(n), 0)),
                   state(dn0), state(ret0)],
        out_shape=[jax.ShapeDtypeStruct((batch, t, mix_w), BF16),
                   jax.ShapeDtypeStruct(dn0.shape, F32),
                   jax.ShapeDtypeStruct(ret0.shape, F32)],
        scratch_shapes=[pltpu.VMEM((2, units, chunk, chunk), F32),
                        pltpu.VMEM((2, units, chunk, chunk), BF16),
                        pltpu.VMEM((2, units, chunk, DN_DV + DN_DK), F32),
                        pltpu.VMEM((2, units, chunk, DN_DK), BF16),
                        pltpu.VMEM((2, units, chunk, DN_DK), BF16),
                        pltpu.VMEM((2, units, 1, LANES), F32)],
        compiler_params=_params("parallel", "arbitrary"),
        name="mixer",
    )(log_gamma, p_qkv.reshape(batch, t, -1), p_gate.reshape(batch, t, LANES),
      p_rest.reshape(batch, t, -1), gate_par, dn_norm, inv_freq, dn0, ret0)
    return mixed.reshape(m, mix_w), dn_n, ret_n


def _ffn_kernel(x_ref, mix_ref, gm_ref, wo_ref, nw_ref, sh_ref, sc_ref, gf_ref, wg_ref, wu_ref, wd_ref,
                nf_ref, shf_ref, scf_ref, y_ref, x1_ref, h_ref):
    j = pl.program_id(1)

    @pl.when(j == 0)
    def _():
        x1 = x_ref[...] + gm_ref[0] * jnp.dot(mix_ref[...], wo_ref[...], preferred_element_type=F32)
        x1_ref[...] = x1
        h = _norm_mod(x1, nw_ref[...], sh_ref[0], sc_ref[0])
        h_ref[...] = h.astype(BF16)
        y_ref[...] = jnp.zeros_like(y_ref)

    hb = h_ref[...]
    gate = jnp.dot(hb, wg_ref[...], preferred_element_type=F32)
    up = jnp.dot(hb, wu_ref[...], preferred_element_type=F32)
    act = (_silu(gate) * up).astype(BF16)
    y_ref[...] += jnp.dot(act, wd_ref[...], preferred_element_type=F32)

    @pl.when(j == pl.num_programs(1) - 1)
    def _():
        x2 = x1_ref[...] + gf_ref[0] * y_ref[...]
        y_ref[...] = _norm_mod(x2, nf_ref[...], shf_ref[0], scf_ref[0])


def _ffn(x, mixed, w_out, nw, ada, mod_map, w_gu, w_down, nf, ada_fin, fin_map, tm, tf):
    m, d = x.shape
    kmix = mixed.shape[1]
    dff = w_down.shape[0]
    nf_blocks = dff // tf
    r = ada.shape[1]
    mod = lambda k: pl.BlockSpec((1, r, d), lambda i, j: (mod_map(i, k), 0, 0))
    fin = lambda k: pl.BlockSpec((1, r, d), lambda i, j: (fin_map(i, k), 0, 0))
    return pl.pallas_call(
        _ffn_kernel,
        grid=(m // tm, nf_blocks),
        in_specs=[pl.BlockSpec((tm, d), lambda i, j: (i, 0)),
                  pl.BlockSpec((tm, kmix), lambda i, j: (i, 0)),
                  mod(2),
                  pl.BlockSpec((kmix, d), lambda i, j: (0, 0), pipeline_mode=pl.Buffered(1)),
                  pl.BlockSpec((1, d), lambda i, j: (0, 0)),
                  mod(3), mod(4), mod(5),
                  pl.BlockSpec((d, tf), lambda i, j: (0, j)),
                  pl.BlockSpec((d, tf), lambda i, j: (0, j + nf_blocks)),
                  pl.BlockSpec((tf, d), lambda i, j: (j, 0)),
                  pl.BlockSpec((1, d), lambda i, j: (0, 0)),
                  fin(0), fin(1)],
        out_specs=pl.BlockSpec((tm, d), lambda i, j: (i, 0)),
        out_shape=jax.ShapeDtypeStruct((m, d), F32),
        scratch_shapes=[pltpu.VMEM((tm, d), F32), pltpu.VMEM((tm, d), BF16)],
        compiler_params=_params("parallel", "arbitrary"),
        name="ffn",
    )(x, mixed, ada, w_out, nw, ada, ada, ada, w_gu, w_gu, w_down, nf, ada_fin, ada_fin)


def _regroup_kernel(*refs):
    o_ref = refs[-1]
    for k, ref in enumerate(refs[:-1]):
        o_ref[:, k * LANES:(k + 1) * LANES] = ref[...].astype(o_ref.dtype)


def _regroup(left, right, kinds, heads):
    d = left.shape[0]
    assert right.shape[1] == kinds * heads * LANES
    piece = lambda k: pl.BlockSpec((d, LANES), lambda g: (0, k * heads + g))
    return pl.pallas_call(
        _regroup_kernel,
        grid=(heads,),
        in_specs=[piece(k) for k in range(kinds)] * 2,
        out_specs=pl.BlockSpec((d, 2 * kinds * LANES), lambda g: (0, g)),
        out_shape=jax.ShapeDtypeStruct((d, 2 * kinds * heads * LANES), BF16),
        compiler_params=_params("parallel"),
        name="regroup",
    )(*([left] * kinds + [right] * kinds))


def _to_blocks(a, heads):
    shp = a.shape
    kinds = shp[-1] // (heads * LANES)
    a = a.reshape(shp[:-1] + (kinds, heads, LANES))
    return jnp.swapaxes(a, -3, -2).reshape(shp)


def _from_blocks(a, heads):
    shp = a.shape
    kinds = shp[-1] // (heads * LANES)
    a = a.reshape(shp[:-1] + (heads, kinds, LANES))
    return jnp.swapaxes(a, -3, -2).reshape(shp)


class _Tiles(NamedTuple):
    proj_rows: int
    ffn_rows: int
    ffn_cols: int


def _stream(x, ada, ada_fin, conv0, dn0, ret0, weights, *, chunk, pos0, tiles, per_row):
    (nw_mix, nw_ffn, w_main, w_gate, conv_w, gate_par, dn_norm, inv_freq, log_gamma,
     w_out, w_gu, w_down, nf) = weights
    b, t, d = x.shape
    m = b * t
    xf = x.reshape(m, d)
    if per_row:
        ada_r = jnp.repeat(ada.reshape(b, N_ADA, d), t, axis=0).transpose(1, 0, 2)
        fin_r = jnp.repeat(ada_fin.reshape(b, 2, d), t, axis=0).transpose(1, 0, 2)
        mod_map = lambda tm: (lambda i, k: k)
        fin_map = lambda tm: (lambda i, k: k)
    else:
        assert t % tiles.proj_rows == 0 and t % tiles.ffn_rows == 0
        ada_r = ada.reshape(b * N_ADA, 1, d)
        fin_r = ada_fin.reshape(b * 2, 1, d)
        mod_map = lambda tm: (lambda i, k: (i // (t // tm)) * N_ADA + k)
        fin_map = lambda tm: (lambda i, k: (i // (t // tm)) * 2 + k)

    p_qkv, p_rest, p_gate, conv_n = _in_proj(xf, nw_mix, ada_r, ada_r, mod_map(tiles.proj_rows),
                                             w_main, w_gate, conv_w, conv0, t, tiles.proj_rows)
    mixed, dn_n, ret_n = _mixer(p_qkv, p_rest, p_gate, gate_par, dn_norm, inv_freq, log_gamma,
                                dn0, ret0, batch=b, chunk=chunk, pos0=pos0)
    y = _ffn(xf, mixed, w_out, nw_ffn, ada_r, mod_map(tiles.ffn_rows), w_gu, w_down, nf, fin_r,
             fin_map(tiles.ffn_rows), tiles.ffn_rows, tiles.ffn_cols)
    return y.reshape(b, t, d), conv_n, dn_n, ret_n


def kernel(x_prompt, x_sample, state_conv, state_delta, state_ret, c_prompt, c_sample, norm_mix, norm_ffn, w_ada, b_ada, w_in, conv_w, dn_a_log, dn_dt_bias, dn_norm, w_out, w_gu, w_down, norm_final, w_ada_final, b_ada_final):
    bp, tp, d = x_prompt.shape
    bs, ts, _ = x_sample.shape
    depth = w_in.shape[0]
    assert depth == 1, "single-layer pipeline"
    dn_heads = state_delta.shape[2]
    ret_heads = state_ret.shape[2]
    conv_ch = conv_w.shape[2]
    assert 2 * dn_heads <= LANES

    nb = bp + bs
    pad = (-nb) % (2 * SUBLANES)
    c_all = jnp.concatenate([c_prompt, c_sample, jnp.zeros((pad, d), F32)], axis=0)
    ada_all = _ada(c_all, w_ada[0], b_ada[0], 1024)
    fin_all = _ada(c_all, w_ada_final, b_ada_final, 1024)

    l = 0
    g_off = conv_ch + dn_heads * DN_DV
    r_off = g_off + 2 * dn_heads
    w_l = w_in[l]
    w_main = _regroup(w_l, w_l[:, r_off:], g_off // (dn_heads * LANES), dn_heads)
    w_gate = jnp.concatenate([w_l[:, g_off:r_off],
                              jnp.zeros((d, LANES - 2 * dn_heads), F32)], axis=1).astype(BF16)
    lane_pad = jnp.zeros((LANES - 2 * dn_heads,), F32)
    gate_par = jnp.stack([
        jnp.concatenate([jnp.zeros((dn_heads,), F32), -jnp.exp(dn_a_log[l].astype(F32)), lane_pad]),
        jnp.concatenate([jnp.zeros((dn_heads,), F32), dn_dt_bias[l].astype(F32), lane_pad])])
    inv_freq = 1.0 / (ROPE_BASE ** jnp.linspace(0.0, 1.0, RET_DK // 2, dtype=F32))
    inv_freq = jnp.stack([-inv_freq, inv_freq], axis=-1).reshape(1, RET_DK)
    log_gamma = jnp.log(1.0 - 2.0 ** (-5.0 - jnp.arange(ret_heads, dtype=F32)))
    weights = (norm_mix[l].reshape(1, d), norm_ffn[l].reshape(1, d), w_main, w_gate,
               _to_blocks(conv_w[l], dn_heads),
               gate_par, dn_norm[l].reshape(1, -1), inv_freq, log_gamma,
               w_out[l].astype(BF16), w_gu[l].astype(BF16), w_down[l].astype(BF16),
               norm_final.reshape(1, d))

    zc = jnp.zeros((bp, CONV_W - 1, conv_ch), F32)
    zd = jnp.zeros((bp,) + state_delta.shape[2:], F32)
    zr = jnp.zeros((bp,) + state_ret.shape[2:], F32)

    y_p, cp, dp, rp = _stream(x_prompt, ada_all[:bp], fin_all[:bp], zc, zd, zr, weights,
                              chunk=CHUNK, pos0=0, per_row=False,
                              tiles=_Tiles(proj_rows=1024, ffn_rows=512, ffn_cols=512))
    y_s, cs, ds, rs = _stream(x_sample, ada_all[bp:nb], fin_all[bp:nb],
                              _to_blocks(state_conv[l], dn_heads), state_delta[l], state_ret[l], weights,
                              chunk=ts, pos0=PAST_LEN, per_row=True,
                              tiles=_Tiles(proj_rows=bs * ts, ffn_rows=bs * ts, ffn_cols=512))
    cp, cs = _from_blocks(cp, dn_heads), _from_blocks(cs, dn_heads)
    return (y_p, y_s, cp[None], dp[None], rp[None], cs[None], ds[None], rs[None])
```

```python
import functools
from typing import NamedTuple

import jax
import jax.numpy as jnp
from jax import lax
from jax.experimental import pallas as pl
from jax.experimental.pallas import tpu as pltpu

F32 = jnp.float32
BF16 = jnp.bfloat16

DN_DK = 128
DN_DV = 128
RET_DK = 256
RET_DV = 256
CONV_W = 4
N_ADA = 6
CHUNK = 64
PAST_LEN = 2048
ROPE_BASE = 10000.0
EPS = 1e-6

LANES = 128
SUBLANES = 8
VMEM_LIMIT = 56 * 1024 * 1024


def _silu(x):
    return x * jax.nn.sigmoid(x)


def _dot(a, b):
    return jnp.dot(a.astype(BF16), b.astype(BF16), preferred_element_type=F32)


def _dot_nt(a, b):
    return lax.dot_general(a.astype(BF16), b.astype(BF16), (((1,), (1,)), ((), ())),
                           preferred_element_type=F32)


def _dot_tn(a, b):
    return lax.dot_general(a.astype(BF16), b.astype(BF16), (((0,), (0,)), ((), ())),
                           preferred_element_type=F32)


def _params(*sem):
    return pltpu.CompilerParams(dimension_semantics=sem, vmem_limit_bytes=VMEM_LIMIT)


def _ada_kernel(c_ref, w_ref, b_ref, o_ref):
    s = _silu(c_ref[...])
    o_ref[...] = _dot(s, w_ref[...]) + b_ref[...]


def _ada(c_all, w, b, tn):
    m, d = c_all.shape
    n = w.shape[1]
    return pl.pallas_call(
        _ada_kernel,
        grid=(n // tn,),
        in_specs=[pl.BlockSpec((m, d), lambda j: (0, 0)),
                  pl.BlockSpec((d, tn), lambda j: (0, j)),
                  pl.BlockSpec((1, tn), lambda j: (0, j))],
        out_specs=pl.BlockSpec((m, tn), lambda j: (0, j)),
        out_shape=jax.ShapeDtypeStruct((m, n), F32),
        compiler_params=_params("arbitrary"),
        name="ada",
    )(c_all, w, b.reshape(1, n))


def _norm_mod(x, nw, shift, scale):
    y = x * lax.rsqrt(jnp.mean(x * x, axis=-1, keepdims=True) + EPS)
    y = y * nw
    return y * (1.0 + scale) + shift


HIST = SUBLANES
QKV_W = 2 * DN_DK + DN_DV
REST_PIECES = 5
REST_W = REST_PIECES * LANES
ROW_PARTS = 4


def _proj_kernel(x_ref, nw_ref, sh_ref, sc_ref, w_ref, wg_ref, cw_ref, conv0_ref,
                 pqkv_ref, prest_ref, pg_ref, convn_ref, h_ref, pe_ref, tail_ref,
                 *, seg, tiles_per_batch):
    i = pl.program_id(0)
    j = pl.program_id(1)
    tm = pqkv_ref.shape[0]
    nseg = tm // seg

    @pl.when(j == 0)
    def _():
        h = _norm_mod(x_ref[...], nw_ref[...], sh_ref[0], sc_ref[0])
        h_ref[...] = h.astype(BF16)
        pg_ref[...] = jnp.dot(h_ref[...], wg_ref[...], preferred_element_type=F32)

    def conv_rows(s, r0, r1):
        n = r1 - r0
        out = pe_ref[s, pl.ds(HIST + r0, n), :] * cw_ref[CONV_W - 1:CONV_W, :]
        for t in range(1, CONV_W):
            out = out + pe_ref[s, pl.ds(HIST + r0 - t, n), :] * cw_ref[CONV_W - 1 - t:CONV_W - t, :]
        out = _silu(out)
        q = out[:, 0:DN_DK]
        k = out[:, DN_DK:2 * DN_DK]
        q = q * lax.rsqrt(jnp.sum(q * q, axis=-1, keepdims=True) + EPS) * (DN_DK ** -0.5)
        k = k * lax.rsqrt(jnp.sum(k * k, axis=-1, keepdims=True) + EPS)
        base = s * seg
        pqkv_ref[base + r0:base + r1, 0:DN_DK] = q
        pqkv_ref[base + r0:base + r1, DN_DK:2 * DN_DK] = k
        pqkv_ref[base + r0:base + r1, 2 * DN_DK:] = out[:, 2 * DN_DK:]

    def start_hist(s):
        hist = jnp.concatenate([jnp.zeros((HIST - (CONV_W - 1), QKV_W), F32), conv0_ref[s]], axis=0)
        if tiles_per_batch > 1:
            hist = jnp.where(i % tiles_per_batch == 0, hist, tail_ref[j])
        pe_ref[s, 0:HIST, :] = hist

    nsub = ROW_PARTS if nseg == 1 else 1
    sub = tm // nsub
    for u in range(nsub):
        acc = jnp.dot(h_ref[u * sub:(u + 1) * sub, :], w_ref[...], preferred_element_type=F32)
        prest_ref[u * sub:(u + 1) * sub, :] = acc[:, QKV_W:]
        for s in range(nseg if nsub == 1 else 1):
            r0 = 0 if nsub == 1 else u * sub
            rows = acc[s * seg:(s + 1) * seg, 0:QKV_W] if nsub == 1 else acc[:, 0:QKV_W]
            if u == 0:
                start_hist(s)
            pe_ref[s, HIST + r0:HIST + r0 + rows.shape[0], :] = rows
            if u == nsub - 1:
                if tiles_per_batch > 1:
                    tail_ref[j] = rows[rows.shape[0] - HIST:]
                convn_ref[s] = rows[rows.shape[0] - (CONV_W - 1):]
            conv_rows(s, r0, r0 + rows.shape[0])


def _in_proj(x, nw, sh, sc, mod_map, w_main, w_gate, conv_w, conv0, t, tm):
    m, d = x.shape
    tn = QKV_W + REST_W
    nblk = w_main.shape[1] // tn
    r = sh.shape[1]
    assert conv_w.shape[1] == nblk * QKV_W
    seg = min(tm, t)
    assert tm % seg == 0 and t % seg == 0 and seg >= HIST
    nseg = tm // seg
    tiles_per_batch = t // seg
    assert nseg == 1 or tiles_per_batch == 1
    batch_blk = (lambda i: i // tiles_per_batch) if nseg == 1 else (lambda i: i)
    kern = functools.partial(_proj_kernel, seg=seg, tiles_per_batch=tiles_per_batch)
    p_qkv, p_rest, p_gate, conv_seg = pl.pallas_call(
        kern,
        grid=(m // tm, nblk),
        in_specs=[pl.BlockSpec((tm, d), lambda i, j: (i, 0)),
                  pl.BlockSpec((1, d), lambda i, j: (0, 0)),
                  pl.BlockSpec((1, r, d), lambda i, j: (mod_map(i, 0), 0, 0)),
                  pl.BlockSpec((1, r, d), lambda i, j: (mod_map(i, 1), 0, 0)),
                  pl.BlockSpec((d, tn), lambda i, j: (0, j)),
                  pl.BlockSpec((d, LANES), lambda i, j: (0, 0)),
                  pl.BlockSpec((CONV_W, QKV_W), lambda i, j: (0, j)),
                  pl.BlockSpec((nseg, CONV_W - 1, QKV_W), lambda i, j: (batch_blk(i), 0, j))],
        out_specs=[pl.BlockSpec((tm, QKV_W), lambda i, j: (i, j)),
                   pl.BlockSpec((tm, REST_W), lambda i, j: (i, j)),
                   pl.BlockSpec((tm, LANES), lambda i, j: (i, 0)),
                   pl.BlockSpec((nseg, CONV_W - 1, QKV_W), lambda i, j: (i, 0, j))],
        out_shape=[jax.ShapeDtypeStruct((m, nblk * QKV_W), F32),
                   jax.ShapeDtypeStruct((m, nblk * REST_W), F32),
                   jax.ShapeDtypeStruct((m, LANES), F32),
                   jax.ShapeDtypeStruct((m // seg,) + conv0.shape[1:], F32)],
        scratch_shapes=[pltpu.VMEM((tm, d), BF16),
                        pltpu.VMEM((nseg, HIST + seg, QKV_W), F32),
                        pltpu.VMEM((nblk, HIST, QKV_W), F32)],
        compiler_params=_params("arbitrary", "arbitrary"),
        name="in_proj",
    )(x, nw, sh, sc, w_main, w_gate, conv_w, conv0)
    conv_n = conv_seg.reshape((m // t, tiles_per_batch) + conv0.shape[1:])[:, -1]
    return p_qkv, p_rest, p_gate, conv_n


def _split(a):
    hi = a.astype(BF16)
    lo = (a - hi.astype(F32)).astype(BF16)
    return hi, lo


def _dot_split(a_parts, b_parts):
    ah, al = a_parts
    bh, bl = b_parts
    rows = ah.shape[0]
    top = jnp.dot(jnp.concatenate([ah, al], axis=0), bh, preferred_element_type=F32)
    return top[:rows] + top[rows:] + jnp.dot(ah, bl, preferred_element_type=F32)


def _neumann_correction(ms, fillers=()):
    c = ms[0].shape[0]
    fillers = list(fillers)

    def fill():
        if fillers:
            fillers.pop(0)()

    ns = [-m for m in ms]
    pws = [_dot_split(_split(m), _split(m)) for m in ms]
    fill()
    k = 2
    while k < c:
        pps = [_split(pw) for pw in pws]
        nps = [_split(n) for n in ns]
        if 2 * k < c:
            tops = [_dot_split((jnp.concatenate([pp[0], npart[0]], axis=0),
                                jnp.concatenate([pp[1], npart[1]], axis=0)), pp)
                    for pp, npart in zip(pps, nps)]
            fill()
            ns = [n + pw + top[c:] for n, pw, top in zip(ns, pws, tops)]
            pws = [top[:c] for top in tops]
        else:
            prods = [_dot_split(npart, pp) for npart, pp in zip(nps, pps)]
            fill()
            ns = [n + pw + prod for n, pw, prod in zip(ns, pws, prods)]
        k *= 2
    while fillers:
        fill()
    return ns


def _mixer_kernel(lg_ref, pqkv_ref, pg_ref, prest_ref, gp_ref, dnw_ref, inv_ref, dn0_ref, ret0_ref,
                  mix_ref, dn_ref, ret_ref,
                  m_s, qkg_s, rhs_s, qd_s, kd_s, cd_s,
                  *, chunk, nb, dn_heads, ret_heads, pos0):
    c = chunk
    n = pl.program_id(1)
    batches = range(nb)
    units = [(b, h) for b in batches for h in range(dn_heads)]
    runits = [(b, h) for b in batches for h in range(ret_heads)]
    uid = lambda u: u[0] * dn_heads + u[1]
    wr = n % 2
    rd = 1 - wr

    @pl.when(n == 0)
    def _():
        m_s[...] = jnp.zeros_like(m_s)
        qkg_s[...] = jnp.zeros_like(qkg_s)
        rhs_s[...] = jnp.zeros_like(rhs_s)
        qd_s[...] = jnp.zeros_like(qd_s)
        kd_s[...] = jnp.zeros_like(kd_s)
        cd_s[...] = jnp.zeros_like(cd_s)
        dn_ref[...] = dn0_ref[...]
        ret_ref[...] = ret0_ref[...]

    row = lax.broadcasted_iota(jnp.int32, (c, c), 0)
    col = lax.broadcasted_iota(jnp.int32, (c, c), 1)
    causal = row >= col
    strict = row > col

    mix_off = dn_heads * DN_DV

    beta_all, cs = [], []
    for b in batches:
        pg = pg_ref[b]
        beta_all.append(jax.nn.sigmoid(pg))
        g_all = gp_ref[0:1, :] * jax.nn.softplus(pg + gp_ref[1:2, :])
        g_hi, g_lo = _split(g_all)
        g_lo2 = (g_all - g_hi.astype(F32) - g_lo.astype(F32)).astype(BF16)
        cs.append(jnp.dot(causal.astype(BF16), jnp.concatenate([g_hi, g_lo, g_lo2], axis=-1),
                          preferred_element_type=F32))

    prep = {}

    def prep_values():
        for b in batches:
            decay_all = cs[b][:, :LANES] + cs[b][:, LANES:2 * LANES] + cs[b][:, 2 * LANES:]
            decay_t = decay_all.T
            for h in range(dn_heads):
                i = uid((b, h))
                q = pqkv_ref[b, :, h * QKV_W:h * QKV_W + DN_DK]
                k = pqkv_ref[b, :, h * QKV_W + DN_DK:h * QKV_W + 2 * DN_DK]
                v = pqkv_ref[b, :, h * QKV_W + 2 * DN_DK:(h + 1) * QKV_W]
                beta = beta_all[b][:, h:h + 1]
                dcol = decay_all[:, dn_heads + h:dn_heads + h + 1]
                drow = decay_t[dn_heads + h:dn_heads + h + 1, :]
                last = drow[:, c - 1:c]
                kb = k * beta
                edec = jnp.exp(dcol)
                prep[i] = dict(k=k, lhs=jnp.concatenate([kb, q], axis=0).astype(BF16),
                               gam=jnp.where(causal, jnp.exp(jnp.where(causal, dcol - drow, 0.0)), 0.0))
                rhs_s[wr, i] = jnp.concatenate([v * beta, kb * edec], axis=-1)
                qd_s[wr, i] = (q * edec).astype(BF16)
                kd_s[wr, i] = (k * jnp.exp(last - dcol)).astype(BF16)
                cd_s[wr, i] = jnp.broadcast_to(jnp.exp(last), (1, LANES))

    def prep_matmuls():
        for u in units:
            p = prep[uid(u)]
            p["both"] = _dot_nt(p["lhs"], p["k"])

    def prep_store():
        for u in units:
            i = uid(u)
            both, gam = prep[i]["both"], prep[i]["gam"]
            m_s[wr, i] = jnp.where(strict, both[:c] * gam, 0.0)
            qkg_s[wr, i] = (both[c:] * gam).astype(BF16)

    tok = lax.broadcasted_iota(jnp.int32, (c, 1), 0)
    idx = tok.astype(F32)
    ret = {}

    def piece(b, blk, kind):
        off = blk * REST_W + kind * LANES
        return prest_ref[b, :, off:off + LANES]

    even_lane = lax.broadcasted_iota(jnp.int32, (c, LANES), 1) % 2 == 0

    def rot(b, h, kind, cos, sin):
        halves = []
        for p in range(2):
            x = piece(b, 2 * h + p, kind)
            partner = jnp.where(even_lane, pltpu.roll(x, LANES - 1, axis=1), pltpu.roll(x, 1, axis=1))
            lanes = slice(p * LANES, (p + 1) * LANES)
            halves.append(x * cos[:, lanes] + partner * sin[:, lanes])
        return jnp.concatenate(halves, axis=-1)

    def wide(b, h, kind):
        return jnp.concatenate([piece(b, 2 * h, kind), piece(b, 2 * h + 1, kind)], axis=-1)

    def ret_values():
        pos = (pos0 + (n - 1) * c + tok).astype(F32)
        ang = pos * inv_ref[...]
        cos = jnp.cos(ang)
        sin = jnp.sin(ang)
        for u in runits:
            b, h = u
            lg = lg_ref[h]
            rq = rot(b, h, 1, cos, sin)
            rk = rot(b, h, 2, cos, sin) * (RET_DK ** -0.5)
            ret[u] = dict(rq=rq.astype(BF16), rk=rk.astype(BF16),
                          rv=wide(b, h, 3).astype(BF16),
                          qx=(rq * jnp.exp((idx + 1.0) * lg)).astype(BF16),
                          kz=(rk * jnp.exp((c - 1.0 - idx) * lg)).astype(BF16))

    def ret_scores():
        for u in runits:
            ret[u]["qk"] = _dot_nt(ret[u]["rq"], ret[u]["rk"])

    def ret_inner():
        dpos = (row - col).astype(F32)
        dmask = [jnp.where(causal, jnp.exp(jnp.where(causal, dpos, 0.0) * lg_ref[h]), 0.0)
                 for h in range(ret_heads)]
        for u in runits:
            ret[u]["inner"] = jnp.dot((ret[u]["qk"] * dmask[u[1]]).astype(BF16), ret[u]["rv"],
                                      preferred_element_type=F32)

    ns = _neumann_correction([m_s[rd, uid(u)] for u in units],
                             [prep_values, prep_matmuls, prep_store, ret_values, ret_scores, ret_inner])
    rhs = [rhs_s[rd, uid(u)] for u in units]
    sols = [r + _dot_split(_split(nn), _split(r)) for r, nn in zip(rhs, ns)]

    sd = [dn_ref[b, h] for b, h in units]
    ws = [jnp.dot(jnp.concatenate([sol[:, DN_DV:].astype(BF16), qd_s[rd, uid(u)]], axis=0),
                  s.astype(BF16), preferred_element_type=F32) for u, sol, s in zip(units, sols, sd)]
    sr = [ret_ref[b, h] for b, h in runits]
    ro = [ret[u]["inner"] + jnp.dot(ret[u]["qx"], s.astype(BF16), preferred_element_type=F32)
          for u, s in zip(runits, sr)]
    rupd = [_dot_tn(ret[u]["kz"], ret[u]["rv"]) for u in runits]
    us = [(sol[:, :DN_DV] - w[:c]).astype(BF16) for sol, w in zip(sols, ws)]
    o2 = [jnp.dot(qkg_s[rd, uid(u)], uu, preferred_element_type=F32) for u, uu in zip(units, us)]
    upd = [_dot_tn(kd_s[rd, uid(u)], uu) for u, uu in zip(units, us)]
    cdr = [jnp.exp(jnp.zeros((1, 1), F32) + c * lg_ref[h]) for h in range(ret_heads)]
    for k, (b, h) in enumerate(runits):
        ret_ref[b, h] = sr[k] * cdr[h] + rupd[k]
        o = ro[k]
        o = o * lax.rsqrt(jnp.mean(o * o, axis=-1, keepdims=True) + EPS)
        gg = wide(b, h, 4)
        mix_ref[b, :, mix_off + h * RET_DV:mix_off + (h + 1) * RET_DV] = (
            (o * _silu(gg)).astype(mix_ref.dtype))
    for k, (b, h) in enumerate(units):
        dn_ref[b, h] = sd[k] * cd_s[rd, uid((b, h))][:, 0:1] + upd[k]
        o = ws[k][c:] + o2[k]
        o = o * lax.rsqrt(jnp.mean(o * o, axis=-1, keepdims=True) + EPS) * dnw_ref[...]
        zz = piece(b, h, 0)
        mix_ref[b, :, h * DN_DV:(h + 1) * DN_DV] = (o * _silu(zz)).astype(mix_ref.dtype)

    @pl.when(n == 0)
    def _():
        dn_ref[...] = dn0_ref[...]
        ret_ref[...] = ret0_ref[...]


def _mixer(p_qkv, p_rest, p_gate, gate_par, dn_norm, inv_freq, log_gamma, dn0, ret0,
           *, batch, chunk, pos0):
    m = p_qkv.shape[0]
    t = m // batch
    nchunk = t // chunk
    dn_heads = dn0.shape[1]
    ret_heads = ret0.shape[1]
    mix_w = dn_heads * DN_DV + ret_heads * RET_DV
    assert p_qkv.shape[1] == dn_heads * QKV_W and p_rest.shape[1] == dn_heads * REST_W
    assert dn_heads == 2 * ret_heads and RET_DK == 2 * LANES and RET_DV == 2 * LANES
    nb = 2 if batch % 2 == 0 else 1
    kern = functools.partial(_mixer_kernel, chunk=chunk, nb=nb, dn_heads=dn_heads,
                             ret_heads=ret_heads, pos0=pos0)
    full = lambda shape: pl.BlockSpec(shape, lambda g, n: (0,) * len(shape))
    cur = lambda n: jnp.minimum(n, nchunk - 1)
    prev = lambda n: jnp.maximum(n - 1, 0)
    state = lambda a: pl.BlockSpec((nb,) + a.shape[1:], lambda g, n: (g,) + (0,) * (a.ndim - 1))
    units = nb * dn_heads
    mixed, dn_n, ret_n = pl.pallas_call(
        kern,
        grid=(batch // nb, nchunk + 1),
        in_specs=[pl.BlockSpec(memory_space=pltpu.SMEM),
                  pl.BlockSpec((nb, chunk, p_qkv.shape[1]), lambda g, n: (g, cur(n), 0)),
                  pl.BlockSpec((nb, chunk, LANES), lambda g, n: (g, cur(n), 0)),
                  pl.BlockSpec((nb, chunk, p_rest.shape[1]), lambda g, n: (g, prev(n), 0)),
                  full(gate_par.shape), full(dn_norm.shape), full(inv_freq.shape),
                  state(dn0), state(ret0)],
        out_specs=[pl.BlockSpec((nb, chunk, mix_w), lambda g, n: (g, prev(n), 0)),
                   state(dn0), state(ret0)],
        out_shape=[jax.ShapeDtypeStruct((batch, t, mix_w), BF16),
                   jax.ShapeDtypeStruct(dn0.shape, F32),
                   jax.ShapeDtypeStruct(ret0.shape, F32)],
        scratch_shapes=[pltpu.VMEM((2, units, chunk, chunk), F32),
                        pltpu.VMEM((2, units, chunk, chunk), BF16),
                        pltpu.VMEM((2, units, chunk, DN_DV + DN_DK), F32),
                        pltpu.VMEM((2, units, chunk, DN_DK), BF16),
                        pltpu.VMEM((2, units, chunk, DN_DK), BF16),
                        pltpu.VMEM((2, units, 1, LANES), F32)],
        compiler_params=_params("parallel", "arbitrary"),
        name="mixer",
    )(log_gamma, p_qkv.reshape(batch, t, -1), p_gate.reshape(batch, t, LANES),
      p_rest.reshape(batch, t, -1), gate_par, dn_norm, inv_freq, dn0, ret0)
    return mixed.reshape(m, mix_w), dn_n, ret_n


def _ffn_kernel(x_ref, mix_ref, gm_ref, wo_ref, nw_ref, sh_ref, sc_ref, gf_ref, wg_ref, wu_ref, wd_ref,
                nf_ref, shf_ref, scf_ref, y_ref, x1_ref, h_ref):
    j = pl.program_id(1)

    @pl.when(j == 0)
    def _():
        x1 = x_ref[...] + gm_ref[0] * jnp.dot(mix_ref[...], wo_ref[...], preferred_element_type=F32)
        x1_ref[...] = x1
        h = _norm_mod(x1, nw_ref[...], sh_ref[0], sc_ref[0])
        h_ref[...] = h.astype(BF16)
        y_ref[...] = jnp.zeros_like(y_ref)

    hb = h_ref[...]
    gate = jnp.dot(hb, wg_ref[...], preferred_element_type=F32)
    up = jnp.dot(hb, wu_ref[...], preferred_element_type=F32)
    act = (_silu(gate) * up).astype(BF16)
    y_ref[...] += jnp.dot(act, wd_ref[...], preferred_element_type=F32)

    @pl.when(j == pl.num_programs(1) - 1)
    def _():
        x2 = x1_ref[...] + gf_ref[0] * y_ref[...]
        y_ref[...] = _norm_mod(x2, nf_ref[...], shf_ref[0], scf_ref[0])


def _ffn(x, mixed, w_out, nw, ada, mod_map, w_gu, w_down, nf, ada_fin, fin_map, tm, tf):
    m, d = x.shape
    kmix = mixed.shape[1]
    dff = w_down.shape[0]
    nf_blocks = dff // tf
    r = ada.shape[1]
    mod = lambda k: pl.BlockSpec((1, r, d), lambda i, j: (mod_map(i, k), 0, 0))
    fin = lambda k: pl.BlockSpec((1, r, d), lambda i, j: (fin_map(i, k), 0, 0))
    return pl.pallas_call(
        _ffn_kernel,
        grid=(m // tm, nf_blocks),
        in_specs=[pl.BlockSpec((tm, d), lambda i, j: (i, 0)),
                  pl.BlockSpec((tm, kmix), lambda i, j: (i, 0)),
                  mod(2),
                  pl.BlockSpec((kmix, d), lambda i, j: (0, 0), pipeline_mode=pl.Buffered(1)),
                  pl.BlockSpec((1, d), lambda i, j: (0, 0)),
                  mod(3), mod(4), mod(5),
                  pl.BlockSpec((d, tf), lambda i, j: (0, j)),
                  pl.BlockSpec((d, tf), lambda i, j: (0, j + nf_blocks)),
                  pl.BlockSpec((tf, d), lambda i, j: (j, 0)),
                  pl.BlockSpec((1, d), lambda i, j: (0, 0)),
                  fin(0), fin(1)],
        out_specs=pl.BlockSpec((tm, d), lambda i, j: (i, 0)),
        out_shape=jax.ShapeDtypeStruct((m, d), F32),
        scratch_shapes=[pltpu.VMEM((tm, d), F32), pltpu.VMEM((tm, d), BF16)],
        compiler_params=_params("parallel", "arbitrary"),
        name="ffn",
    )(x, mixed, ada, w_out, nw, ada, ada, ada, w_gu, w_gu, w_down, nf, ada_fin, ada_fin)


def _regroup_kernel(*refs):
    o_ref = refs[-1]
    for k, ref in enumerate(refs[:-1]):
        o_ref[:, k * LANES:(k + 1) * LANES] = ref[...].astype(o_ref.dtype)


def _regroup(left, right, kinds, heads):
    d = left.shape[0]
    assert right.shape[1] == kinds * heads * LANES
    piece = lambda k: pl.BlockSpec((d, LANES), lambda g: (0, k * heads + g))
    return pl.pallas_call(
        _regroup_kernel,
        grid=(heads,),
        in_specs=[piece(k) for k in range(kinds)] * 2,
        out_specs=pl.BlockSpec((d, 2 * kinds * LANES), lambda g: (0, g)),
        out_shape=jax.ShapeDtypeStruct((d, 2 * kinds * heads * LANES), BF16),
        compiler_params=_params("parallel"),
        name="regroup",
    )(*([left] * kinds + [right] * kinds))


def _to_blocks(a, heads):
    shp = a.shape
    kinds = shp[-1] // (heads * LANES)
    a = a.reshape(shp[:-1] + (kinds, heads, LANES))
    return jnp.swapaxes(a, -3, -2).reshape(shp)


def _from_blocks(a, heads):
    shp = a.shape
    kinds = shp[-1] // (heads * LANES)
    a = a.reshape(shp[:-1] + (heads, kinds, LANES))
    return jnp.swapaxes(a, -3, -2).reshape(shp)


class _Tiles(NamedTuple):
    proj_rows: int
    ffn_rows: int
    ffn_cols: int


def _stream(x, ada, ada_fin, conv0, dn0, ret0, weights, *, chunk, pos0, tiles, per_row):
    (nw_mix, nw_ffn, w_main, w_gate, conv_w, gate_par, dn_norm, inv_freq, log_gamma,
     w_out, w_gu, w_down, nf) = weights
    b, t, d = x.shape
    m = b * t
    xf = x.reshape(m, d)
    if per_row:
        ada_r = jnp.repeat(ada.reshape(b, N_ADA, d), t, axis=0).transpose(1, 0, 2)
        fin_r = jnp.repeat(ada_fin.reshape(b, 2, d), t, axis=0).transpose(1, 0, 2)
        mod_map = lambda tm: (lambda i, k: k)
        fin_map = lambda tm: (lambda i, k: k)
    else:
        assert t % tiles.proj_rows == 0 and t % tiles.ffn_rows == 0
        ada_r = ada.reshape(b * N_ADA, 1, d)
        fin_r = ada_fin.reshape(b * 2, 1, d)
        mod_map = lambda tm: (lambda i, k: (i // (t // tm)) * N_ADA + k)
        fin_map = lambda tm: (lambda i, k: (i // (t // tm)) * 2 + k)

    p_qkv, p_rest, p_gate, conv_n = _in_proj(xf, nw_mix, ada_r, ada_r, mod_map(tiles.proj_rows),
                                             w_main, w_gate, conv_w, conv0, t, tiles.proj_rows)
    mixed, dn_n, ret_n = _mixer(p_qkv, p_rest, p_gate, gate_par, dn_norm, inv_freq, log_gamma,
                                dn0, ret0, batch=b, chunk=chunk, pos0=pos0)
    y = _ffn(xf, mixed, w_out, nw_ffn, ada_r, mod_map(tiles.ffn_rows), w_gu, w_down, nf, fin_r,
             fin_map(tiles.ffn_rows), tiles.ffn_rows, tiles.ffn_cols)
    return y.reshape(b, t, d), conv_n, dn_n, ret_n


def kernel(x_prompt, x_sample, state_conv, state_delta, state_ret, c_prompt, c_sample, norm_mix, norm_ffn, w_ada, b_ada, w_in, conv_w, dn_a_log, dn_dt_bias, dn_norm, w_out, w_gu, w_down, norm_final, w_ada_final, b_ada_final):
    bp, tp, d = x_prompt.shape
    bs, ts, _ = x_sample.shape
    depth = w_in.shape[0]
    assert depth == 1, "single-layer pipeline"
    dn_heads = state_delta.shape[2]
    ret_heads = state_ret.shape[2]
    conv_ch = conv_w.shape[2]
    assert 2 * dn_heads <= LANES

    nb = bp + bs
    pad = (-nb) % (2 * SUBLANES)
    c_all = jnp.concatenate([c_prompt, c_sample, jnp.zeros((pad, d), F32)], axis=0)
    ada_all = _ada(c_all, w_ada[0], b_ada[0], 1024)
    fin_all = _ada(c_all, w_ada_final, b_ada_final, 1024)

    l = 0
    g_off = conv_ch + dn_heads * DN_DV
    r_off = g_off + 2 * dn_heads
    w_l = w_in[l]
    w_main = _regroup(w_l, w_l[:, r_off:], g_off // (dn_heads * LANES), dn_heads)
    w_gate = jnp.concatenate([w_l[:, g_off:r_off],
                              jnp.zeros((d, LANES - 2 * dn_heads), F32)], axis=1).astype(BF16)
    lane_pad = jnp.zeros((LANES - 2 * dn_heads,), F32)
    gate_par = jnp.stack([
        jnp.concatenate([jnp.zeros((dn_heads,), F32), -jnp.exp(dn_a_log[l].astype(F32)), lane_pad]),
        jnp.concatenate([jnp.zeros((dn_heads,), F32), dn_dt_bias[l].astype(F32), lane_pad])])
    inv_freq = 1.0 / (ROPE_BASE ** jnp.linspace(0.0, 1.0, RET_DK // 2, dtype=F32))
    inv_freq = jnp.stack([-inv_freq, inv_freq], axis=-1).reshape(1, RET_DK)
    log_gamma = jnp.log(1.0 - 2.0 ** (-5.0 - jnp.arange(ret_heads, dtype=F32)))
    weights = (norm_mix[l].reshape(1, d), norm_ffn[l].reshape(1, d), w_main, w_gate,
               _to_blocks(conv_w[l], dn_heads),
               gate_par, dn_norm[l].reshape(1, -1), inv_freq, log_gamma,
               w_out[l].astype(BF16), w_gu[l].astype(BF16), w_down[l].astype(BF16),
               norm_final.reshape(1, d))

    zc = jnp.zeros((bp, CONV_W - 1, conv_ch), F32)
    zd = jnp.zeros((bp,) + state_delta.shape[2:], F32)
    zr = jnp.zeros((bp,) + state_ret.shape[2:], F32)

    y_p, cp, dp, rp = _stream(x_prompt, ada_all[:bp], fin_all[:bp], zc, zd, zr, weights,
                              chunk=CHUNK, pos0=0, per_row=False,
                              tiles=_Tiles(proj_rows=1024, ffn_rows=512, ffn_cols=512))
    y_s, cs, ds, rs = _stream(x_sample, ada_all[bp:nb], fin_all[bp:nb],
                              _to_blocks(state_conv[l], dn_heads), state_delta[l], state_ret[l], weights,
                              chunk=ts, pos0=PAST_LEN, per_row=True,
                              tiles=_Tiles(proj_rows=bs * ts, ffn_rows=bs * ts, ffn_cols=512))
    cp, cs = _from_blocks(cp, dn_heads), _from_blocks(cs, dn_heads)
    return (y_p, y_s, cp[None], dp[None], rp[None], cs[None], ds[None], rs[None])
```
